```python
import functools
import jax, jax.numpy as jnp
from jax import lax
import numpy as np

D_MODEL = 2048
BATCH = 4
SEQ = 2048
DEPTH = 4
DEC_BATCH = 8
DEC_SEQ = 4
PAST_LEN = 16384
PAGE_SIZE = 128

D_CONV = D_MODEL // 2
CONV_WIDTH = 31
N_HEADS = 8
HEAD_DIM = 128
D_ATT = N_HEADS * HEAD_DIM
Q_BLOCK = 128
PEER_HEADS = 8
N_KEYS = 128
N_EXPERTS = N_KEYS * N_KEYS
D_KEY = 256
PEER_TOPK = 16
PEER_CHUNK = 128
EPS = 1e-6
D_IN = 2 * D_CONV + 3 * D_ATT + N_HEADS + 2 * D_MODEL

kernel_name = "hybrid_conv_fox_peer_adaln_step"


def _rmsnorm(x, g):
    xf = x.astype(jnp.float32)
    y = xf * lax.rsqrt(jnp.mean(xf * xf, axis=-1, keepdims=True) + EPS)
    return (y * g.astype(jnp.float32)).astype(x.dtype)


def _layernorm(x, g, b):
    xf = x.astype(jnp.float32)
    mu = jnp.mean(xf, axis=-1, keepdims=True)
    xc = xf - mu
    y = xc * lax.rsqrt(jnp.mean(xc * xc, axis=-1, keepdims=True) + EPS)
    return (y * g.astype(jnp.float32) + b.astype(jnp.float32)).astype(x.dtype)


def _modulate(x, g, shift, scale):
    return _rmsnorm(x, g) * (1 + scale[:, None, :]) + shift[:, None, :]


def _project_in(h, w_in, b_f, g_q, g_k):
    B, T, _ = h.shape
    z = h @ w_in
    c1 = D_CONV
    c2 = 2 * D_CONV
    c3 = c2 + D_ATT
    c4 = c3 + D_ATT
    c5 = c4 + D_ATT
    c6 = c5 + N_HEADS
    c7 = c6 + D_MODEL
    a_lin, a_gate, q, k, v, f_logit, g_a, g_b = jnp.split(z, [c1, c2, c3, c4, c5, c6, c7], axis=-1)
    a = a_lin * jax.nn.sigmoid(a_gate)
    q = _rmsnorm(q.reshape(B, T, N_HEADS, HEAD_DIM), g_q)
    k = _rmsnorm(k.reshape(B, T, N_HEADS, HEAD_DIM), g_k)
    v = v.reshape(B, T, N_HEADS, HEAD_DIM)
    logf = jax.nn.log_sigmoid((f_logit + b_f).astype(jnp.float32))
    return a, q, k, v, logf, g_a, g_b


def _conv_branch(a_ext, conv_w, ln_g, ln_b, w_a_proj):
    y = lax.conv_general_dilated(
        a_ext, conv_w[:, None, :].astype(a_ext.dtype), window_strides=(1,), padding='VALID',
        dimension_numbers=('NWC', 'WIO', 'NWC'), feature_group_count=a_ext.shape[-1])
    y = jax.nn.silu(_layernorm(y, ln_g, ln_b))
    return y @ w_a_proj


def _fox_prompt(q, k, v, logf):
    B, S, H, Dh = q.shape
    nb = S // Q_BLOCK
    scale = HEAD_DIM ** -0.5
    F = jnp.cumsum(logf, axis=1)
    Fk = F.transpose(0, 2, 1)
    qb = q.reshape(B, nb, Q_BLOCK, H, Dh).transpose(1, 0, 2, 3, 4)
    Fq = F.reshape(B, nb, Q_BLOCK, H).transpose(1, 0, 3, 2)
    kpos = jnp.arange(S)

    def block(args):
        qi, fqi, i = args
        s = jnp.einsum('bqhd,bkhd->bhqk', qi, k).astype(jnp.float32) * scale
        s = s + fqi[..., None] - Fk[:, :, None, :]
        qpos = i * Q_BLOCK + jnp.arange(Q_BLOCK)
        s = jnp.where(kpos[None, :] <= qpos[:, None], s, -jnp.inf)
        p = jax.nn.softmax(s, axis=-1).astype(v.dtype)
        return jnp.einsum('bhqk,bkhd->bqhd', p, v)

    o = lax.map(block, (qb, Fq, jnp.arange(nb)))
    return o.transpose(1, 0, 2, 3, 4).reshape(B, S, H, Dh)


def _fox_sample(q, k, v, logf, k_past, v_past, logf_past):
    P = k_past.shape[1]
    T = q.shape[1]
    scale = HEAD_DIM ** -0.5
    lp = logf_past.astype(jnp.float32)
    Fq = jnp.cumsum(logf, axis=1)
    past_after = lax.cumsum(lp, axis=1, reverse=True) - lp
    kb = jnp.concatenate([past_after, -Fq], axis=1)
    k_all = jnp.concatenate([k_past.astype(k.dtype), k], axis=1)
    v_all = jnp.concatenate([v_past.astype(v.dtype), v], axis=1)
    s = jnp.einsum('bqhd,bkhd->bhqk', q, k_all).astype(jnp.float32) * scale
    s = s + Fq.transpose(0, 2, 1)[..., None] + kb.transpose(0, 2, 1)[:, :, None, :]
    kpos = jnp.arange(P + T)
    qpos = P + jnp.arange(T)
    s = jnp.where(kpos[None, :] <= qpos[:, None], s, -jnp.inf)
    p = jax.nn.softmax(s, axis=-1).astype(v.dtype)
    return jnp.einsum('bhqk,bkhd->bqhd', p, v_all)


def _peer(h, w_pq, sub_keys, peer_u, peer_v):
    B, T, D = h.shape
    n = B * T
    hf = h.reshape(n, D)
    q = (hf @ w_pq).reshape(n, PEER_HEADS, 2, D_KEY // 2)
    s = jnp.einsum('nhpd,hpkd->nhpk', q, sub_keys).astype(jnp.float32)
    s1, i1 = lax.top_k(s[:, :, 0], PEER_TOPK)
    s2, i2 = lax.top_k(s[:, :, 1], PEER_TOPK)
    cand = (s1[..., :, None] + s2[..., None, :]).reshape(n, PEER_HEADS, PEER_TOPK * PEER_TOPK)
    cidx = (i1[..., :, None] * N_KEYS + i2[..., None, :]).reshape(n, PEER_HEADS, PEER_TOPK * PEER_TOPK)
    sc, pos = lax.top_k(cand, PEER_TOPK)
    idx = jnp.take_along_axis(cidx, pos, axis=-1).reshape(n, PEER_HEADS * PEER_TOPK)
    g = jax.nn.softmax(sc, axis=-1).reshape(n, PEER_HEADS * PEER_TOPK).astype(h.dtype)
    pad = (-n) % PEER_CHUNK
    m = (n + pad) // PEER_CHUNK
    hp = jnp.pad(hf, ((0, pad), (0, 0))).reshape(m, PEER_CHUNK, D)
    ip = jnp.pad(idx, ((0, pad), (0, 0))).reshape(m, PEER_CHUNK, -1)
    gp = jnp.pad(g, ((0, pad), (0, 0))).reshape(m, PEER_CHUNK, -1)

    def chunk(args):
        hc, ic, gc = args
        act = jax.nn.gelu(jnp.einsum('cd,ced->ce', hc, peer_u[ic]), approximate=False)
        return jnp.einsum('ce,ced->cd', gc * act, peer_v[ic])

    out = lax.map(chunk, (hp, ip, gp))
    return out.reshape(m * PEER_CHUNK, D)[:n].reshape(B, T, D)


def _layer(x, c, conv_hist, attend, lw):
    B, T, _ = x.shape
    mod = jax.nn.silu(c) @ lw['w_ada'] + lw['b_ada']
    sh1, sc1, gt1, sh2, sc2, gt2 = jnp.split(mod, 6, axis=-1)
    h = _modulate(x, lw['g_norm1'], sh1, sc1)
    a, q, k, v, logf, g_a, g_b = _project_in(h, lw['w_in'], lw['b_f'], lw['g_q'], lw['g_k'])
    a_ext = jnp.concatenate([conv_hist.astype(a.dtype), a], axis=1)
    ya = _conv_branch(a_ext, lw['conv_w'], lw['conv_ln_g'], lw['conv_ln_b'], lw['w_a_proj'])
    yb = attend(q, k, v, logf).reshape(B, T, D_ATT) @ lw['w_b_proj']
    mix = (jax.nn.sigmoid(g_a) * ya + jax.nn.sigmoid(g_b) * yb) @ lw['w_out']
    x = x + gt1[:, None, :] * mix
    h2 = _modulate(x, lw['g_norm2'], sh2, sc2)
    x = x + gt2[:, None, :] * _peer(h2, lw['w_pq'], lw['sub_keys'], lw['peer_u'], lw['peer_v'])
    new_conv = a_ext[:, -(CONV_WIDTH - 1):]
    return x, k, v, logf, new_conv


def setup_inputs(seed: int = 0) -> dict:
    key = jax.random.key(seed)
    ks = iter(jax.random.split(key, 40))

    def nrm(shape, s):
        return jax.random.normal(next(ks), shape, jnp.float32) * s

    n_pages = PAST_LEN // PAGE_SIZE
    n_used = DEC_BATCH * n_pages
    n_pool = n_used + max(1, n_used // 4)
    page_table = jax.random.permutation(next(ks), n_pool)[:n_used].reshape(DEC_BATCH, n_pages).astype(jnp.int32)
    return {
        'x_prompt': nrm((BATCH, SEQ, D_MODEL), 1.0),
        'x_sample': nrm((DEC_BATCH, DEC_SEQ, D_MODEL), 1.0),
        'c_prompt': nrm((BATCH, D_MODEL), 1.0),
        'c_sample': nrm((DEC_BATCH, D_MODEL), 1.0),
        'cache_k': nrm((DEPTH, n_pool, PAGE_SIZE, N_HEADS, HEAD_DIM), 1.0),
        'cache_v': nrm((DEPTH, n_pool, PAGE_SIZE, N_HEADS, HEAD_DIM), 1.0),
        'cache_logf': jax.nn.log_sigmoid(3.0 + nrm((DEPTH, n_pool, PAGE_SIZE, N_HEADS), 1.0)),
        'state_conv': nrm((DEPTH, DEC_BATCH, CONV_WIDTH - 1, D_CONV), 0.5),
        'page_table': page_table,
        'w_ada': nrm((DEPTH, D_MODEL, 6 * D_MODEL), 0.5 * D_MODEL ** -0.5),
        'b_ada': nrm((DEPTH, 6 * D_MODEL), 0.01),
        'g_norm1': 1.0 + nrm((DEPTH, D_MODEL), 0.05),
        'g_norm2': 1.0 + nrm((DEPTH, D_MODEL), 0.05),
        'w_in': nrm((DEPTH, D_MODEL, D_IN), D_MODEL ** -0.5),
        'b_f': 3.0 + nrm((DEPTH, N_HEADS), 0.5),
        'g_q': 1.0 + nrm((DEPTH, HEAD_DIM), 0.05),
        'g_k': 1.0 + nrm((DEPTH, HEAD_DIM), 0.05),
        'conv_w': nrm((DEPTH, CONV_WIDTH, D_CONV), CONV_WIDTH ** -0.5),
        'conv_ln_g': 1.0 + nrm((DEPTH, D_CONV), 0.05),
        'conv_ln_b': nrm((DEPTH, D_CONV), 0.02),
        'w_a_proj': nrm((DEPTH, D_CONV, D_MODEL), D_CONV ** -0.5),
        'w_b_proj': nrm((DEPTH, D_ATT, D_MODEL), D_ATT ** -0.5),
        'w_out': nrm((DEPTH, D_MODEL, D_MODEL), D_MODEL ** -0.5),
        'w_pq': nrm((DEPTH, D_MODEL, PEER_HEADS * D_KEY), D_MODEL ** -0.5),
        'sub_keys': nrm((DEPTH, PEER_HEADS, 2, N_KEYS, D_KEY // 2), (D_KEY // 2) ** -0.5),
        'peer_u': nrm((DEPTH, N_EXPERTS, D_MODEL), D_MODEL ** -0.5),
        'peer_v': nrm((DEPTH, N_EXPERTS, D_MODEL), PEER_HEADS ** -0.5),
    }


def reference(x_prompt, x_sample, c_prompt, c_sample, cache_k, cache_v, cache_logf, state_conv,
              page_table, w_ada, b_ada, g_norm1, g_norm2, w_in, b_f, g_q, g_k, conv_w, conv_ln_g,
              conv_ln_b, w_a_proj, w_b_proj, w_out, w_pq, sub_keys, peer_u, peer_v):
    Bp = x_prompt.shape[0]
    Bd = x_sample.shape[0]
    past = page_table.shape[1] * cache_k.shape[2]
    zero_hist = jnp.zeros((Bp, CONV_WIDTH - 1, D_CONV), x_prompt.dtype)
    xp, xs = x_prompt, x_sample
    kp_l, vp_l, fp_l, cp_l = [], [], [], []
    ks_l, vs_l, fs_l, cs_l = [], [], [], []
    for l in range(DEPTH):
        lw = dict(w_ada=w_ada[l], b_ada=b_ada[l], g_norm1=g_norm1[l], g_norm2=g_norm2[l],
                  w_in=w_in[l], b_f=b_f[l], g_q=g_q[l], g_k=g_k[l], conv_w=conv_w[l],
                  conv_ln_g=conv_ln_g[l], conv_ln_b=conv_ln_b[l], w_a_proj=w_a_proj[l],
                  w_b_proj=w_b_proj[l], w_out=w_out[l], w_pq=w_pq[l], sub_keys=sub_keys[l],
                  peer_u=peer_u[l], peer_v=peer_v[l])
        xp, kp, vp, fp, cp = _layer(xp, c_prompt, zero_hist, _fox_prompt, lw)
        kp_l.append(kp); vp_l.append(vp); fp_l.append(fp); cp_l.append(cp)
        k_past = cache_k[l][page_table].reshape(Bd, past, N_HEADS, HEAD_DIM)
        v_past = cache_v[l][page_table].reshape(Bd, past, N_HEADS, HEAD_DIM)
        f_past = cache_logf[l][page_table].reshape(Bd, past, N_HEADS)
        attend = functools.partial(_fox_sample, k_past=k_past, v_past=v_past, logf_past=f_past)
        xs, ksn, vsn, fsn, csn = _layer(xs, c_sample, state_conv[l], attend, lw)
        ks_l.append(ksn); vs_l.append(vsn); fs_l.append(fsn); cs_l.append(csn)
    return (xp, xs, jnp.stack(kp_l), jnp.stack(vp_l), jnp.stack(fp_l), jnp.stack(cp_l),
            jnp.stack(ks_l), jnp.stack(vs_l), jnp.stack(fs_l), jnp.stack(cs_l))
```

```python
import functools
import math

import jax
import jax.numpy as jnp
from jax import lax
from jax.experimental import pallas as pl
from jax.experimental.pallas import tpu as pltpu

F32 = jnp.float32
BF16 = jnp.bfloat16

D_MODEL = 2048
D_CONV = 1024
CONV_WIDTH = 31
N_HEADS = 8
HEAD_DIM = 128
D_ATT = N_HEADS * HEAD_DIM
PEER_HEADS = 8
N_KEYS = 128
PEER_TOPK = 16
EPS = 1e-6

LANES = 128
SUBLANES = 8
VMEM_LIMIT = 56 * 1024 * 1024

_NT = (((1,), (1,)), ((), ()))
_TN = (((0,), (0,)), ((), ()))


def _params(*sem):
    return pltpu.CompilerParams(dimension_semantics=sem, vmem_limit_bytes=VMEM_LIMIT)


def _row_spec(arr, tm, tiles_per_group, width, col=None):
    r = arr.shape[1]
    if col is None:
        return pl.BlockSpec((None, r, width), lambda i, *_: (i // tiles_per_group, 0, 0))
    return pl.BlockSpec((None, r, width), lambda i, j: (i // tiles_per_group, 0, j))


def _ada_kernel(c_ref, w_ref, b_ref, o_ref):
    c = c_ref[...]
    s = (c * jax.nn.sigmoid(c)).astype(BF16)
    o_ref[...] = jnp.dot(s, w_ref[...].astype(BF16), preferred_element_type=F32) + b_ref[...]


def _ada_call(c, w_ada, b_ada):
    depth, d, n6 = w_ada.shape
    rows = c.shape[0]
    tn = 1024
    return pl.pallas_call(
        _ada_kernel,
        grid=(depth, n6 // tn),
        in_specs=[
            pl.BlockSpec((rows, d), lambda l, j: (0, 0)),
            pl.BlockSpec((None, d, tn), lambda l, j: (l, 0, j)),
            pl.BlockSpec((None, 1, tn), lambda l, j: (l, 0, j)),
        ],
        out_specs=pl.BlockSpec((None, rows, tn), lambda l, j: (l, 0, j)),
        out_shape=jax.ShapeDtypeStruct((depth, rows, n6), F32),
        compiler_params=_params("parallel", "parallel"),
        name="ada_mod",
    )(c, w_ada, b_ada.reshape(depth, 1, n6))


def _modulated(x, g, sc, sh):
    y = x * lax.rsqrt(jnp.mean(x * x, axis=-1, keepdims=True) + EPS) * g
    return y * (1.0 + sc) + sh


def _norm_kernel(x_ref, g_ref, sc_ref, sh_ref, h_ref):
    h_ref[...] = _modulated(x_ref[...], g_ref[...], sc_ref[...], sh_ref[...]).astype(h_ref.dtype)


def _resid_norm_kernel(x_ref, y_ref, gt_ref, g_ref, sc_ref, sh_ref, xn_ref, h_ref):
    x = x_ref[...] + gt_ref[...] * y_ref[...]
    xn_ref[...] = x
    h_ref[...] = _modulated(x, g_ref[...], sc_ref[...], sh_ref[...]).astype(h_ref.dtype)


def _resid_kernel(x_ref, y_ref, gt_ref, xn_ref):
    xn_ref[...] = x_ref[...] + gt_ref[...] * y_ref[...]


def _token_tile(n, pref):
    return pref if n % pref == 0 else n


def _norm_call(x, g, sc, sh, tokens_per_group, pending=None):
    n, d = x.shape
    tm = _token_tile(n, 512)
    tpg = max(tokens_per_group // tm, 1)
    tile = pl.BlockSpec((tm, d), lambda i: (i, 0))
    gspec = pl.BlockSpec((1, d), lambda i: (0, 0))
    if pending is None:
        return None, pl.pallas_call(
            _norm_kernel,
            grid=(n // tm,),
            in_specs=[tile, gspec, _row_spec(sc, tm, tpg, d), _row_spec(sh, tm, tpg, d)],
            out_specs=tile,
            out_shape=jax.ShapeDtypeStruct((n, d), BF16),
            compiler_params=_params("parallel"),
            name="norm_mod",
        )(x, g, sc, sh)
    y, gt = pending
    return pl.pallas_call(
        _resid_norm_kernel,
        grid=(n // tm,),
        in_specs=[tile, tile, _row_spec(gt, tm, tpg, d), gspec,
                  _row_spec(sc, tm, tpg, d), _row_spec(sh, tm, tpg, d)],
        out_specs=(tile, tile),
        out_shape=(jax.ShapeDtypeStruct((n, d), F32), jax.ShapeDtypeStruct((n, d), BF16)),
        compiler_params=_params("parallel"),
        name="resid_norm_mod",
    )(x, y, gt, g, sc, sh)


def _resid_call(x, y, gt, tokens_per_group):
    n, d = x.shape
    tm = _token_tile(n, 512)
    tpg = max(tokens_per_group // tm, 1)
    tile = pl.BlockSpec((tm, d), lambda i: (i, 0))
    return pl.pallas_call(
        _resid_kernel,
        grid=(n // tm,),
        in_specs=[tile, tile, _row_spec(gt, tm, tpg, d)],
        out_specs=tile,
        out_shape=jax.ShapeDtypeStruct((n, d), F32),
        compiler_params=_params("parallel"),
        name="resid",
    )(x, y, gt)


MM_TM = 1024
MM_TN = 512


def _glu_kernel(h_ref, wl_ref, wg_ref, o_ref):
    h = h_ref[...]
    lin = jnp.dot(h, wl_ref[...], preferred_element_type=F32)
    gate = jnp.dot(h, wg_ref[...], preferred_element_type=F32)
    o_ref[...] = lin * jax.nn.sigmoid(gate)


def _glu_call(h, w_in):
    n, d = h.shape
    tm, tn = _token_tile(n, MM_TM), MM_TN
    goff = D_CONV // tn
    return pl.pallas_call(
        _glu_kernel,
        grid=(n // tm, D_CONV // tn),
        in_specs=[
            pl.BlockSpec((tm, d), lambda i, j: (i, 0)),
            pl.BlockSpec((d, tn), lambda i, j: (0, j)),
            pl.BlockSpec((d, tn), lambda i, j: (0, j + goff)),
        ],
        out_specs=pl.BlockSpec((tm, tn), lambda i, j: (i, j)),
        out_shape=jax.ShapeDtypeStruct((n, D_CONV), F32),
        compiler_params=_params("parallel", "arbitrary"),
        name="proj_glu",
    )(h, w_in, w_in)


def _head_proj_kernel(h_ref, w_ref, g_ref, *o_refs, norm, scale):
    z = jnp.dot(h_ref[...], w_ref[...], preferred_element_type=F32)
    for c in range(z.shape[1] // HEAD_DIM):
        cs = slice(c * HEAD_DIM, (c + 1) * HEAD_DIM)
        zc = z[:, cs]
        if norm:
            zc = zc * lax.rsqrt(jnp.mean(zc * zc, axis=-1, keepdims=True) + EPS) * g_ref[...]
        for o_ref in o_refs:
            o_ref[:, cs] = (zc * scale).astype(BF16) if o_ref.dtype == BF16 else zc


def _head_proj_call(h, w_in, col0, g, *, norm, scale, out_dtypes):
    n, d = h.shape
    tm, tn = _token_tile(n, MM_TM), MM_TN
    j0 = col0 // tn
    tile = pl.BlockSpec((tm, tn), lambda i, j: (i, j))
    outs = pl.pallas_call(
        functools.partial(_head_proj_kernel, norm=norm, scale=scale),
        grid=(n // tm, D_ATT // tn),
        in_specs=[
            pl.BlockSpec((tm, d), lambda i, j: (i, 0)),
            pl.BlockSpec((d, tn), lambda i, j: (0, j + j0)),
            pl.BlockSpec((1, HEAD_DIM), lambda i, j: (0, 0)),
        ],
        out_specs=tuple(tile for _ in out_dtypes),
        out_shape=tuple(jax.ShapeDtypeStruct((n, D_ATT), dt) for dt in out_dtypes),
        compiler_params=_params("parallel", "arbitrary"),
        name="proj_heads",
    )(h, w_in, g)
    return outs


def _logf_kernel(h_ref, w_ref, b_ref, o_ref):
    z = jnp.dot(h_ref[...], w_ref[...], preferred_element_type=F32) + b_ref[...]
    o_ref[...] = jnp.minimum(z, 0.0) - jnp.log(1.0 + jnp.exp(-jnp.abs(z)))


def _logf_call(h, wf, bf):
    n, d = h.shape
    tm = _token_tile(n, MM_TM)
    return pl.pallas_call(
        _logf_kernel,
        grid=(n // tm,),
        in_specs=[
            pl.BlockSpec((tm, d), lambda i: (i, 0)),
            pl.BlockSpec((d, LANES), lambda i: (0, 0)),
            pl.BlockSpec((1, LANES), lambda i: (0, 0)),
        ],
        out_specs=pl.BlockSpec((tm, LANES), lambda i: (i, 0)),
        out_shape=jax.ShapeDtypeStruct((n, LANES), F32),
        compiler_params=_params("parallel"),
        name="proj_logf",
    )(h, wf, bf)


_HALO = 32


def _conv_kernel(a_ref, ap_ref, hist_ref, w_ref, g_ref, b_ref, o_ref, newc_ref, ext_ref, *, tt, n_tiles):
    i = pl.program_id(1)
    nh = CONV_WIDTH - 1
    ext_ref[0:_HALO - nh, :] = jnp.zeros((_HALO - nh, D_CONV), F32)
    ext_ref[_HALO:_HALO + tt, :] = a_ref[...]
    if n_tiles == 1:
        ext_ref[_HALO - nh:_HALO, :] = hist_ref[...]
    else:
        @pl.when(i == 0)
        def _():
            ext_ref[_HALO - nh:_HALO, :] = hist_ref[...]

        @pl.when(i > 0)
        def _():
            ext_ref[_HALO - nh:_HALO, :] = ap_ref[tt - nh:tt, :]

    cols = []
    for c in range(D_CONV // LANES):
        c0 = c * LANES
        win = ext_ref[:, c0:c0 + LANES]
        acc = jnp.zeros((tt, LANES), F32)
        for s in range(SUBLANES):
            sh = win if s == 0 else win[s:s + tt + _HALO - SUBLANES, :]
            for q in range(_HALO // SUBLANES + 1):
                j = SUBLANES * q + s - (_HALO - nh)
                if 0 <= j < CONV_WIDTH:
                    acc = acc + sh[SUBLANES * q:SUBLANES * q + tt, :] * w_ref[j:j + 1, c0:c0 + LANES]
        cols.append(acc)
    y = jnp.concatenate(cols, axis=1)
    mu = jnp.mean(y, axis=-1, keepdims=True)
    yc = y - mu
    z = yc * lax.rsqrt(jnp.mean(yc * yc, axis=-1, keepdims=True) + EPS) * g_ref[...] + b_ref[...]
    o_ref[...] = (z * jax.nn.sigmoid(z)).astype(o_ref.dtype)

    @pl.when(i == n_tiles - 1)
    def _():
        newc_ref[...] = ext_ref[tt + _HALO - nh:tt + _HALO, :]


def _conv_call(a, hist, conv_w, ln_g, ln_b):
    b, t, dc = a.shape
    tt = 256 if t % 256 == 0 else t
    n_tiles = t // tt
    nh = CONV_WIDTH - 1
    return pl.pallas_call(
        functools.partial(_conv_kernel, tt=tt, n_tiles=n_tiles),
        grid=(b, n_tiles),
        in_specs=[
            pl.BlockSpec((None, tt, dc), lambda bi, i: (bi, i, 0)),
            pl.BlockSpec((None, tt, dc), lambda bi, i: (bi, jnp.maximum(i - 1, 0), 0)),
            pl.BlockSpec((None, nh, dc), lambda bi, i: (bi, 0, 0)),
            pl.BlockSpec((CONV_WIDTH, dc), lambda bi, i: (0, 0)),
            pl.BlockSpec((1, dc), lambda bi, i: (0, 0)),
            pl.BlockSpec((1, dc), lambda bi, i: (0, 0)),
        ],
        out_specs=(
            pl.BlockSpec((None, tt, dc), lambda bi, i: (bi, i, 0)),
            pl.BlockSpec((None, nh, dc), lambda bi, i: (bi, 0, 0)),
        ),
        out_shape=(jax.ShapeDtypeStruct((b, t, dc), BF16), jax.ShapeDtypeStruct((b, nh, dc), F32)),
        scratch_shapes=[pltpu.VMEM((tt + _HALO, dc), F32)],
        compiler_params=_params("parallel", "arbitrary"),
        name="conv_branch",
    )(a, a, hist, conv_w, ln_g, ln_b)


_CS_BLK = 256


def _cumsum_kernel(x_ref, o_ref):
    t = x_ref.shape[0]
    r = lax.broadcasted_iota(jnp.int32, (_CS_BLK, _CS_BLK), 0)
    c = lax.broadcasted_iota(jnp.int32, (_CS_BLK, _CS_BLK), 1)
    tri = (c <= r).astype(F32)
    carry = jnp.zeros((1, x_ref.shape[1]), F32)
    for blk in range(t // _CS_BLK):
        rows = slice(blk * _CS_BLK, (blk + 1) * _CS_BLK)
        f = jnp.dot(tri, x_ref[rows, :], precision=lax.Precision.HIGHEST, preferred_element_type=F32) + carry
        o_ref[rows, :] = f
        carry = f[_CS_BLK - 1:_CS_BLK, :]


def _cumsum_call(x):
    b, t, w = x.shape
    return pl.pallas_call(
        _cumsum_kernel,
        grid=(b,),
        in_specs=[pl.BlockSpec((None, t, w), lambda bi: (bi, 0, 0))],
        out_specs=pl.BlockSpec((None, t, w), lambda bi: (bi, 0, 0)),
        out_shape=jax.ShapeDtypeStruct((b, t, w), F32),
        compiler_params=_params("parallel"),
        name="logf_cumsum",
    )(x)


ATT_TQ = 512


def _attn_kernel(q_ref, k_ref, v_ref, fq_ref, fk_ref, o_ref, *, tq):
    i = pl.program_id(2)
    q = q_ref[...]
    fq = fq_ref[...]
    qpos = i * tq + lax.broadcasted_iota(jnp.int32, (tq, tq), 0)
    kidx = lax.broadcasted_iota(jnp.int32, (tq, tq), 1)

    def body(j, carry):
        m, l, acc = carry
        ks = pl.multiple_of(j * tq, tq)
        k = k_ref[pl.ds(ks, tq), :]
        v = v_ref[pl.ds(ks, tq), :]
        s = lax.dot_general(q, k, _NT, preferred_element_type=F32)
        s = s + fq - fk_ref[:, pl.ds(ks, tq)]
        s = jnp.where(ks + kidx <= qpos, s, -jnp.inf)
        m_new = jnp.maximum(m, jnp.max(s, axis=1, keepdims=True))
        p = jnp.exp(s - m_new)
        alpha = jnp.exp(m - m_new)
        l = alpha * l + jnp.sum(p, axis=1, keepdims=True)
        acc = alpha * acc + jnp.dot(p.astype(BF16), v, preferred_element_type=F32)
        return m_new, l, acc

    init = (jnp.full((tq, 1), -jnp.inf, F32), jnp.zeros((tq, 1), F32), jnp.zeros((tq, HEAD_DIM), F32))
    _, l, acc = lax.fori_loop(0, i + 1, body, init)
    o_ref[...] = (acc / l).astype(o_ref.dtype)


def _attn_prompt_call(q, k, v, f_col, f_row, batch, t):
    tq = ATT_TQ
    nq = t // tq
    return pl.pallas_call(
        functools.partial(_attn_kernel, tq=tq),
        grid=(batch, N_HEADS, nq),
        in_specs=[
            pl.BlockSpec((tq, HEAD_DIM), lambda b, h, i: (b * nq + i, h)),
            pl.BlockSpec((t, HEAD_DIM), lambda b, h, i: (b, h)),
            pl.BlockSpec((t, HEAD_DIM), lambda b, h, i: (b, h)),
            pl.BlockSpec((None, None, tq, 1), lambda b, h, i: (b, h, i, 0)),
            pl.BlockSpec((None, None, 1, t), lambda b, h, i: (b, h, 0, 0)),
        ],
        out_specs=pl.BlockSpec((tq, HEAD_DIM), lambda b, h, i: (b * nq + i, h)),
        out_shape=jax.ShapeDtypeStruct((batch * t, D_ATT), BF16),
        compiler_params=_params("parallel", "parallel", "arbitrary"),
        name="fox_prompt",
    )(q, k, v, f_col, f_row)


ATT_PAGES = 8
NEW_PAD = 16


def _sample_attn_kernel(pt_ref, qbd_ref, kn_ref, vn_ref, lfn_ref, *rest, n_new, n_steps):
    kp = rest[0:ATT_PAGES]
    vp = rest[ATT_PAGES:2 * ATT_PAGES]
    lp = rest[2 * ATT_PAGES:3 * ATT_PAGES]
    o_ref, m_ref, l_ref, acc_ref, carry_ref, fq_ref = rest[3 * ATT_PAGES:]
    g = pl.program_id(1)
    cols = qbd_ref.shape[1]
    page = kp[0].shape[0]

    row8 = lax.broadcasted_iota(jnp.int32, (NEW_PAD, cols), 0)
    tcol = lax.broadcasted_iota(jnp.int32, (NEW_PAD, cols), 1) // N_HEADS

    @pl.when(g == 0)
    def _():
        fs = jnp.zeros((NEW_PAD, cols), F32)
        for t in range(n_new):
            fs = fs + jnp.where(row8 >= t, lfn_ref[t:t + 1, :], 0.0)
        fq_ref[0:NEW_PAD, :] = fs
        fq_ref[NEW_PAD:NEW_PAD + 1, :] = jnp.sum(jnp.where(row8 == tcol, fs, 0.0), axis=0, keepdims=True)
        m_ref[...] = jnp.full(m_ref.shape, -jnp.inf, F32)
        l_ref[...] = jnp.zeros(l_ref.shape, F32)
        acc_ref[...] = jnp.zeros(acc_ref.shape, F32)
        carry_ref[...] = jnp.zeros(carry_ref.shape, F32)

    qbd = qbd_ref[...]
    fq = fq_ref[NEW_PAD:NEW_PAD + 1, :]

    def update(s, v_bf):
        m_old = m_ref[...]
        m_new = jnp.maximum(m_old, jnp.max(s, axis=0, keepdims=True))
        p = jnp.exp(s - m_new)
        alpha = jnp.exp(m_old - m_new)
        l_ref[...] = alpha * l_ref[...] + jnp.sum(p, axis=0, keepdims=True)
        acc_ref[...] = alpha * acc_ref[...] + lax.dot_general(
            v_bf, p.astype(BF16), _TN, preferred_element_type=F32)
        m_ref[...] = m_new

    pr = lax.broadcasted_iota(jnp.int32, (page, page), 0)
    pc = lax.broadcasted_iota(jnp.int32, (page, page), 1)
    tri = (pc > pr).astype(F32)
    for r in range(ATT_PAGES):
        lpv = lp[r][...]
        after = jnp.dot(tri, lpv, precision=lax.Precision.HIGHEST, preferred_element_type=F32)
        carry = carry_ref[...]
        s = jnp.dot(kp[r][...].astype(BF16), qbd, preferred_element_type=F32)
        s = s + (after + carry) + fq
        update(s, vp[r][...].astype(BF16))
        carry_ref[...] = carry + after[0:1, :] + lpv[0:1, :]

    @pl.when(g == n_steps - 1)
    def _():
        s = jnp.dot(kn_ref[...], qbd, preferred_element_type=F32)
        s = s + fq - fq_ref[0:NEW_PAD, :]
        s = jnp.where((row8 <= tcol) & (row8 < n_new), s, -jnp.inf)
        update(s, vn_ref[...])
        o_ref[...] = acc_ref[...] / l_ref[...]


def _sample_attn_call(page_table, qbd, k_new, v_new, lf_new, cache_k, cache_v, cache_lf, n_new):
    bd, n_pages = page_table.shape
    page = cache_k.shape[1]
    cols = qbd.shape[2]
    n_steps = n_pages // ATT_PAGES

    def page_map(r):
        return lambda b, g, pt: (pt[b, n_pages - 1 - (g * ATT_PAGES + r)], 0, 0)

    per_b = lambda b, g, pt: (b, 0, 0)
    in_specs = [
        pl.BlockSpec((None, D_ATT, cols), per_b),
        pl.BlockSpec((None, NEW_PAD, D_ATT), per_b),
        pl.BlockSpec((None, NEW_PAD, D_ATT), per_b),
        pl.BlockSpec((None, NEW_PAD, cols), per_b),
    ]
    in_specs += [pl.BlockSpec((None, page, D_ATT), page_map(r)) for r in range(ATT_PAGES)]
    in_specs += [pl.BlockSpec((None, page, D_ATT), page_map(r)) for r in range(ATT_PAGES)]
    in_specs += [pl.BlockSpec((None, page, cols), page_map(r)) for r in range(ATT_PAGES)]
    grid_spec = pltpu.PrefetchScalarGridSpec(
        num_scalar_prefetch=1,
        grid=(bd, n_steps),
        in_specs=in_specs,
        out_specs=pl.BlockSpec((None, D_ATT, cols), per_b),
        scratch_shapes=[
            pltpu.VMEM((1, cols), F32),
            pltpu.VMEM((1, cols), F32),
            pltpu.VMEM((D_ATT, cols), F32),
            pltpu.VMEM((1, cols), F32),
            pltpu.VMEM((2 * NEW_PAD, cols), F32),
        ],
    )
    return pl.pallas_call(
        functools.partial(_sample_attn_kernel, n_new=n_new, n_steps=n_steps),
        grid_spec=grid_spec,
        out_shape=jax.ShapeDtypeStruct((bd, D_ATT, cols), F32),
        compiler_params=_params("parallel", "arbitrary"),
        name="fox_sample",
    )(page_table, qbd, k_new, v_new, lf_new,
      *([cache_k] * ATT_PAGES), *([cache_v] * ATT_PAGES), *([cache_lf] * ATT_PAGES))


def _gated_kernel(h_ref, wg_ref, a_ref, w_ref, *rest):
    gate = jax.nn.sigmoid(jnp.dot(h_ref[...], wg_ref[...], preferred_element_type=F32))
    y = gate * jnp.dot(a_ref[...], w_ref[...], preferred_element_type=F32)
    if len(rest) == 2:
        y = y + rest[0][...]
    o_ref = rest[-1]
    o_ref[...] = y.astype(o_ref.dtype)


def _gated_call(h, w_gates, gate_col0, a, w, prev, out_dtype):
    n, d = h.shape
    ka = a.shape[1]
    tm, tn = _token_tile(n, MM_TM), MM_TN
    j0 = gate_col0 // tn
    tile = pl.BlockSpec((tm, tn), lambda i, j: (i, j))
    in_specs = [
        pl.BlockSpec((tm, d), lambda i, j: (i, 0)),
        pl.BlockSpec((d, tn), lambda i, j: (0, j + j0)),
        pl.BlockSpec((tm, ka), lambda i, j: (i, 0)),
        pl.BlockSpec((ka, tn), lambda i, j: (0, j)),
    ]
    args = [h, w_gates, a, w]
    if prev is not None:
        in_specs.append(tile)
        args.append(prev)
    return pl.pallas_call(
        _gated_kernel,
        grid=(n // tm, D_MODEL // tn),
        in_specs=in_specs,
        out_specs=tile,
        out_shape=jax.ShapeDtypeStruct((n, D_MODEL), out_dtype),
        compiler_params=_params("parallel", "arbitrary"),
        name="gated_branch",
    )(*args)


def _out_proj_kernel(m_ref, w_ref, x_ref, gt_ref, o_ref):
    o_ref[...] = x_ref[...] + gt_ref[...] * jnp.dot(m_ref[...], w_ref[...], preferred_element_type=F32)


def _out_proj_call(m, w_out, x, gt, tokens_per_group):
    n, d = m.shape
    tm, tn = _token_tile(n, MM_TM), MM_TN
    tpg = max(tokens_per_group // tm, 1)
    tile = pl.BlockSpec((tm, tn), lambda i, j: (i, j))
    return pl.pallas_call(
        _out_proj_kernel,
        grid=(n // tm, D_MODEL // tn),
        in_specs=[
            pl.BlockSpec((tm, d), lambda i, j: (i, 0)),
            pl.BlockSpec((d, tn), lambda i, j: (0, j)),
            tile,
            _row_spec(gt, tm, tpg, tn, col=True),
        ],
        out_specs=tile,
        out_shape=jax.ShapeDtypeStruct((n, D_MODEL), F32),
        compiler_params=_params("parallel", "arbitrary"),
        name="out_proj",
    )(m, w_out, x, gt)


def _peer_score_kernel(h_ref, w_ref, sk_ref, o_ref):
    z = jnp.dot(h_ref[...], w_ref[...], preferred_element_type=F32).astype(BF16)
    for p in range(2):
        zp = z[:, p * N_KEYS:(p + 1) * N_KEYS]
        o_ref[p] = lax.dot_general(sk_ref[p], zp, _NT, preferred_element_type=F32)


def _peer_score_call(h, w_pq, sub_keys):
    n, d = h.shape
    tm = _token_tile(n, MM_TM)
    dk = 2 * N_KEYS
    return pl.pallas_call(
        _peer_score_kernel,
        grid=(n // tm, PEER_HEADS),
        in_specs=[
            pl.BlockSpec((tm, d), lambda i, j: (i, 0)),
            pl.BlockSpec((d, dk), lambda i, j: (0, j)),
            pl.BlockSpec((2, N_KEYS, N_KEYS), lambda i, j: (j, 0, 0)),
        ],
        out_specs=pl.BlockSpec((2, N_KEYS, tm), lambda i, j: (j, 0, i)),
        out_shape=jax.ShapeDtypeStruct((2 * PEER_HEADS, N_KEYS, n), F32),
        compiler_params=_params("parallel", "arbitrary"),
        name="peer_scores",
    )(h, w_pq, sub_keys)


def _knock_out_max(s, rid):
    mx = jnp.max(s, axis=0, keepdims=True)
    pos = jnp.min(jnp.where(s == mx, rid, s.shape[0]), axis=0, keepdims=True)
    return mx, jnp.where(rid == pos, -jnp.inf, s)


def _top_desc(s, k):
    rid = lax.broadcasted_iota(jnp.int32, s.shape, 0)
    rk = lax.broadcasted_iota(jnp.int32, (k, s.shape[1]), 0)
    tops = jnp.full((k, s.shape[1]), -jnp.inf, F32)
    for r in range(k):
        mx, s = _knock_out_max(s, rid)
        tops = jnp.where(rk == r, mx, tops)
    return tops


def _peer_topk_kernel(st_ref, tau_ref, c1_ref, p2_ref):
    cw = st_ref.shape[2]
    k = PEER_TOPK
    row8 = lax.broadcasted_iota(jnp.int32, (SUBLANES, cw), 0)

    def head(h, carry):
        s1 = st_ref[2 * h]
        s2 = st_ref[2 * h + 1]
        t1 = _top_desc(s1, k)
        t2 = _top_desc(s2, k)
        blocks = [t2 + t1[0:1, :], t2[0:SUBLANES, :] + t1[1:2, :]]
        for a in range(2, SUBLANES):
            blocks.append(jnp.where(row8 < k // (a + 1), t2[0:SUBLANES, :] + t1[a:a + 1, :], -jnp.inf))
        blocks.append(t1[SUBLANES:k, :] + t2[0:1, :])
        cand = jnp.concatenate(blocks, axis=0)
        rid = lax.broadcasted_iota(jnp.int32, cand.shape, 0)
        work = cand
        for _ in range(k):
            tau, work = _knock_out_max(work, rid)
        m1 = t1[0:1, :]
        m2 = t2[0:1, :]
        z = jnp.sum(jnp.where(cand >= tau, jnp.exp(cand - (m1 + m2)), 0.0), axis=0, keepdims=True)
        tau_ref[pl.ds(h, 1), :] = tau
        c1_ref[h] = jnp.exp(s1 - m1) / z
        p2_ref[h] = jnp.exp(s2 - m2)
        return carry

    lax.fori_loop(0, PEER_HEADS, head, 0)


def _peer_topk_call(st):
    n = st.shape[2]
    cw = LANES if n % LANES == 0 else n
    return pl.pallas_call(
        _peer_topk_kernel,
        grid=(n // cw,),
        in_specs=[pl.BlockSpec((2 * PEER_HEADS, N_KEYS, cw), lambda i: (0, 0, i))],
        out_specs=(
            pl.BlockSpec((PEER_HEADS, cw), lambda i: (0, i)),
            pl.BlockSpec((PEER_HEADS, N_KEYS, cw), lambda i: (0, 0, i)),
            pl.BlockSpec((PEER_HEADS, N_KEYS, cw), lambda i: (0, 0, i)),
        ),
        out_shape=(
            jax.ShapeDtypeStruct((PEER_HEADS, n), F32),
            jax.ShapeDtypeStruct((PEER_HEADS, N_KEYS, n), F32),
            jax.ShapeDtypeStruct((PEER_HEADS, N_KEYS, n), F32),
        ),
        compiler_params=_params("parallel"),
        name="peer_topk",
    )(st)


PEER_TM = 512
PEER_E1 = 8

_INV_SQRT2 = 0.7071067811865476


def _peer_mix_kernel(h_ref, u_ref, v_ref, st_ref, tau_ref, c1_ref, p2_ref, o_ref, act_ref, wact_ref, *, cw):
    kblk = pl.program_id(1)
    tm = h_ref.shape[0]
    n_chunks = tm // cw
    act_ref[...] = lax.dot_general(u_ref[...], h_ref[...], _NT, preferred_element_type=F32)

    def body(g, carry):
        e1 = kblk * PEER_E1 + g
        r0 = pl.multiple_of(g * N_KEYS, N_KEYS)
        s1_rows = [st_ref[2 * h, pl.ds(e1, 1), :] for h in range(PEER_HEADS)]
        c1_rows = [c1_ref[h, pl.ds(e1, 1), :] for h in range(PEER_HEADS)]
        for ch in range(n_chunks):
            cs = slice(ch * cw, (ch + 1) * cw)
            w = jnp.zeros((N_KEYS, cw), F32)
            for h in range(PEER_HEADS):
                s = st_ref[2 * h + 1, :, cs] + s1_rows[h][:, cs]
                val = p2_ref[h, :, cs] * c1_rows[h][:, cs]
                w = w + jnp.where(s >= tau_ref[h:h + 1, cs], val, 0.0)
            a = act_ref[pl.ds(r0, N_KEYS), cs]
            gelu = 0.5 * a * (1.0 + lax.erf(a * _INV_SQRT2))
            wact_ref[pl.ds(r0, N_KEYS), cs] = (w * gelu).astype(BF16)
        return carry

    lax.fori_loop(0, PEER_E1, body, 0)
    y = lax.dot_general(wact_ref[...], v_ref[...], _TN, preferred_element_type=F32)

    @pl.when(kblk == 0)
    def _():
        o_ref[...] = y

    @pl.when(kblk > 0)
    def _():
        o_ref[...] += y


def _peer_mix_call(h, u, v, st, tau, c1, p2):
    n, d = h.shape
    n_exp = u.shape[0]
    tm = _token_tile(n, PEER_TM)
    cw = LANES if tm % LANES == 0 else tm
    eb = PEER_E1 * N_KEYS
    return pl.pallas_call(
        functools.partial(_peer_mix_kernel, cw=cw),
        grid=(n // tm, n_exp // eb),
        in_specs=[
            pl.BlockSpec((tm, d), lambda i, k: (i, 0)),
            pl.BlockSpec((eb, d), lambda i, k: (k, 0)),
            pl.BlockSpec((eb, d), lambda i, k: (k, 0)),
            pl.BlockSpec((2 * PEER_HEADS, N_KEYS, tm), lambda i, k: (0, 0, i)),
            pl.BlockSpec((PEER_HEADS, tm), lambda i, k: (0, i)),
            pl.BlockSpec((PEER_HEADS, N_KEYS, tm), lambda i, k: (0, 0, i)),
            pl.BlockSpec((PEER_HEADS, N_KEYS, tm), lambda i, k: (0, 0, i)),
        ],
        out_specs=pl.BlockSpec((tm, d), lambda i, k: (i, 0)),
        out_shape=jax.ShapeDtypeStruct((n, d), F32),
        scratch_shapes=[pltpu.VMEM((eb, tm), F32), pltpu.VMEM((eb, tm), BF16)],
        compiler_params=_params("parallel", "arbitrary"),
        name="peer_mix",
    )(h, u, v, st, tau, c1, p2)


def _layer(x, pending, mods, hist, lw, batch, t, attend):
    sh1, sc1, gt1, sh2, sc2, gt2 = mods
    if pending is None:
        _, h1 = _norm_call(x, lw["g_norm1"], sc1, sh1, t)
    else:
        x, h1 = _norm_call(x, lw["g_norm1"], sc1, sh1, t, pending)
    a = _glu_call(h1, lw["w_in"])
    scale = HEAD_DIM ** -0.5
    (q_bf,) = _head_proj_call(h1, lw["w_in"], 2 * D_CONV, lw["g_q"], norm=True, scale=scale, out_dtypes=(BF16,))
    k_f32, k_bf = _head_proj_call(h1, lw["w_in"], 2 * D_CONV + D_ATT, lw["g_k"], norm=True, scale=1.0,
                                  out_dtypes=(F32, BF16))
    v_f32, v_bf = _head_proj_call(h1, lw["w_in"], 2 * D_CONV + 2 * D_ATT, lw["g_k"], norm=False, scale=1.0,
                                  out_dtypes=(F32, BF16))
    logf = _logf_call(h1, lw["w_f"], lw["b_f"])
    act, new_conv = _conv_call(a.reshape(batch, t, D_CONV), hist, lw["conv_w"], lw["conv_ln_g"], lw["conv_ln_b"])
    o = attend(q_bf, k_bf, v_bf, logf)
    part = _gated_call(h1, lw["w_gates"], 0, act.reshape(batch * t, D_CONV), lw["w_a_proj"], None, F32)
    mix = _gated_call(h1, lw["w_gates"], D_MODEL, o, lw["w_b_proj"], part, BF16)
    x = _out_proj_call(mix, lw["w_out"], x, gt1, t)
    _, h2 = _norm_call(x, lw["g_norm2"], sc2, sh2, t)
    st = _peer_score_call(h2, lw["w_pq"], lw["sub_keys"])
    tau, c1, p2 = _peer_topk_call(st)
    y = _peer_mix_call(h2, lw["peer_u"], lw["peer_v"], st, tau, c1, p2)
    return x, (y, gt2), k_f32, v_f32, logf[:, :N_HEADS], new_conv


def kernel(x_prompt, x_sample, c_prompt, c_sample, cache_k, cache_v, cache_logf, state_conv, page_table, w_ada, b_ada, g_norm1, g_norm2, w_in, b_f, g_q, g_k, conv_w, conv_ln_g, conv_ln_b, w_a_proj, w_b_proj, w_out, w_pq, sub_keys, peer_u, peer_v):
    bp, seq, d = x_prompt.shape
    bd, tdec, _ = x_sample.shape
    depth = w_ada.shape[0]
    n_pool, page = cache_k.shape[1], cache_k.shape[2]
    cols = tdec * N_HEADS

    rows = bp + bd
    rows_pad = -(-rows // SUBLANES) * SUBLANES
    c_all = jnp.concatenate([c_prompt, c_sample, jnp.zeros((rows_pad - rows, d), F32)], axis=0)
    mod = _ada_call(c_all, w_ada, b_ada)

    c_f = 2 * D_CONV + 3 * D_ATT
    w_in_bf = w_in.astype(BF16)
    w_gates = w_in[:, :, c_f + N_HEADS:].astype(BF16)
    w_f = jnp.pad(w_in[:, :, c_f:c_f + N_HEADS], ((0, 0), (0, 0), (0, LANES - N_HEADS))).astype(BF16)
    b_f_pad = jnp.pad(b_f, ((0, 0), (0, LANES - N_HEADS))).reshape(depth, 1, LANES)
    w_a_bf, w_b_bf, w_out_bf, w_pq_bf = (w.astype(BF16) for w in (w_a_proj, w_b_proj, w_out, w_pq))
    sk_bf = sub_keys.reshape(depth, 2 * PEER_HEADS, N_KEYS, N_KEYS).astype(BF16)
    u_bf = peer_u.astype(BF16)
    v_bf = peer_v.astype(BF16)

    ck = cache_k.reshape(depth, n_pool, page, D_ATT)
    cv = cache_v.reshape(depth, n_pool, page, D_ATT)
    clf = jnp.tile(cache_logf, (1, 1, 1, tdec))
    eye_h = jnp.eye(N_HEADS, dtype=F32)

    def attend_prompt(q_bf, k_bf, v_bf, logf):
        f = _cumsum_call(logf.reshape(bp, seq, LANES))[:, :, :N_HEADS]
        f_t = f.transpose(0, 2, 1)
        return _attn_prompt_call(q_bf, k_bf, v_bf, f_t[..., None], f_t[:, :, None, :], bp, seq)

    def make_attend_sample(l):
        def attend_sample(q_bf, k_bf, v_bf, logf):
            q4 = q_bf.reshape(bd, tdec, N_HEADS, HEAD_DIM).astype(F32)
            qbd = jnp.einsum("bthd,hg->bhdtg", q4, eye_h).reshape(bd, D_ATT, cols).astype(BF16)
            padn = ((0, 0), (0, NEW_PAD - tdec), (0, 0))
            kn = jnp.pad(k_bf.reshape(bd, tdec, D_ATT), padn)
            vn = jnp.pad(v_bf.reshape(bd, tdec, D_ATT), padn)
            lfn = jnp.pad(jnp.tile(logf[:, :N_HEADS].reshape(bd, tdec, N_HEADS), (1, 1, tdec)), padn)
            o_t = _sample_attn_call(page_table, qbd, kn, vn, lfn, ck[l], cv[l], clf[l], tdec)
            o5 = o_t.reshape(bd, N_HEADS, HEAD_DIM, tdec, N_HEADS)
            o = jnp.einsum("bhdth->bthd", o5)
            return o.reshape(bd * tdec, D_ATT).astype(BF16)
        return attend_sample

    xp = x_prompt.reshape(bp * seq, d)
    xs = x_sample.reshape(bd * tdec, d)
    pend_p = pend_s = None
    zero_hist = jnp.zeros((bp, CONV_WIDTH - 1, D_CONV), F32)
    outs_p, outs_s = [], []
    for l in range(depth):
        lw = dict(
            g_norm1=g_norm1[l][None], g_norm2=g_norm2[l][None], w_in=w_in_bf[l], w_gates=w_gates[l],
            w_f=w_f[l], b_f=b_f_pad[l], g_q=g_q[l][None], g_k=g_k[l][None], conv_w=conv_w[l],
            conv_ln_g=conv_ln_g[l][None], conv_ln_b=conv_ln_b[l][None], w_a_proj=w_a_bf[l],
            w_b_proj=w_b_bf[l], w_out=w_out_bf[l], w_pq=w_pq_bf[l], sub_keys=sk_bf[l],
            peer_u=u_bf[l], peer_v=v_bf[l])
        mods_p = tuple(m[:, None, :] for m in jnp.split(mod[l, :bp], 6, axis=-1))
        mods_s = tuple(jnp.repeat(m, tdec, axis=0)[None] for m in jnp.split(mod[l, bp:rows], 6, axis=-1))
        xp, pend_p, kp, vp, fp, cp = _layer(xp, pend_p, mods_p, zero_hist, lw, bp, seq, attend_prompt)
        xs, pend_s, ks, vs, fs, cs = _layer(xs, pend_s, mods_s, state_conv[l], lw, bd, tdec, make_attend_sample(l))
        outs_p.append((kp.reshape(bp, seq, N_HEADS, HEAD_DIM), vp.reshape(bp, seq, N_HEADS, HEAD_DIM),
                       fp.reshape(bp, seq, N_HEADS), cp))
        outs_s.append((ks.reshape(bd, tdec, N_HEADS, HEAD_DIM), vs.reshape(bd, tdec, N_HEADS, HEAD_DIM),
                       fs.reshape(bd, tdec, N_HEADS), cs))
    xp = _resid_call(xp, pend_p[0], pend_p[1], seq).reshape(bp, seq, d)
    xs = _resid_call(xs, pend_s[0], pend_s[1], tdec).reshape(bd, tdec, d)
    stack = lambda outs, i: jnp.stack([o[i] for o in outs])
    return (xp, xs, stack(outs_p, 0), stack(outs_p, 1), stack(outs_p, 2), stack(outs_p, 3),
            stack(outs_s, 0), stack(outs_s, 1), stack(outs_s, 2), stack(outs_s, 3))
```

```python
import functools

import jax
import jax.numpy as jnp
from jax import lax
from jax.experimental import pallas as pl
from jax.experimental.pallas import tpu as pltpu

F32 = jnp.float32
BF16 = jnp.bfloat16

D_MODEL = 2048
D_CONV = 1024
CONV_WIDTH = 31
N_HEADS = 8
HEAD_DIM = 128
D_ATT = N_HEADS * HEAD_DIM
PEER_HEADS = 8
N_KEYS = 128
PEER_TOPK = 16
EPS = 1e-6

LANES = 128
SUBLANES = 8
VMEM_LIMIT = 56 * 1024 * 1024

_NT = (((1,), (1,)), ((), ()))
_TN = (((0,), (0,)), ((), ()))


def _params(*sem):
    return pltpu.CompilerParams(dimension_semantics=sem, vmem_limit_bytes=VMEM_LIMIT)


def _row_spec(arr, tm, tiles_per_group, width, col=None):
    r = arr.shape[1]
    if col is None:
        return pl.BlockSpec((None, r, width), lambda i, *_: (i // tiles_per_group, 0, 0))
    return pl.BlockSpec((None, r, width), lambda i, j: (i // tiles_per_group, 0, j))


def _ada_kernel(c_ref, w_ref, b_ref, o_ref):
    c = c_ref[...]
    s = (c * jax.nn.sigmoid(c)).astype(BF16)
    o_ref[...] = jnp.dot(s, w_ref[...].astype(BF16), preferred_element_type=F32) + b_ref[...]


def _ada_call(c, w_ada, b_ada):
    depth, d, n6 = w_ada.shape
    rows = c.shape[0]
    tn = 1024
    return pl.pallas_call(
        _ada_kernel,
        grid=(depth, n6 // tn),
        in_specs=[
            pl.BlockSpec((rows, d), lambda l, j: (0, 0)),
            pl.BlockSpec((None, d, tn), lambda l, j: (l, 0, j)),
            pl.BlockSpec((None, 1, tn), lambda l, j: (l, 0, j)),
        ],
        out_specs=pl.BlockSpec((None, rows, tn), lambda l, j: (l, 0, j)),
        out_shape=jax.ShapeDtypeStruct((depth, rows, n6), F32),
        compiler_params=_params("parallel", "parallel"),
        name="ada_mod",
    )(c, w_ada, b_ada.reshape(depth, 1, n6))


def _modulated(x, g, sc, sh):
    y = x * lax.rsqrt(jnp.mean(x * x, axis=-1, keepdims=True) + EPS) * g
    return y * (1.0 + sc) + sh


def _norm_kernel(x_ref, g_ref, sc_ref, sh_ref, h_ref):
    h_ref[...] = _modulated(x_ref[...], g_ref[...], sc_ref[...], sh_ref[...]).astype(h_ref.dtype)


def _resid_norm_kernel(x_ref, y_ref, gt_ref, g_ref, sc_ref, sh_ref, xn_ref, h_ref):
    x = x_ref[...] + gt_ref[...] * y_ref[...]
    xn_ref[...] = x
    h_ref[...] = _modulated(x, g_ref[...], sc_ref[...], sh_ref[...]).astype(h_ref.dtype)


def _resid_kernel(x_ref, y_ref, gt_ref, xn_ref):
    xn_ref[...] = x_ref[...] + gt_ref[...] * y_ref[...]


def _token_tile(n, pref):
    return pref if n % pref == 0 else n


def _norm_call(x, g, sc, sh, tokens_per_group, pending=None):
    n, d = x.shape
    tm = _token_tile(n, 512)
    tpg = max(tokens_per_group // tm, 1)
    tile = pl.BlockSpec((tm, d), lambda i: (i, 0))
    gspec = pl.BlockSpec((1, d), lambda i: (0, 0))
    if pending is None:
        return None, pl.pallas_call(
            _norm_kernel,
            grid=(n // tm,),
            in_specs=[tile, gspec, _row_spec(sc, tm, tpg, d), _row_spec(sh, tm, tpg, d)],
            out_specs=tile,
            out_shape=jax.ShapeDtypeStruct((n, d), BF16),
            compiler_params=_params("parallel"),
            name="norm_mod",
        )(x, g, sc, sh)
    y, gt = pending
    return pl.pallas_call(
        _resid_norm_kernel,
        grid=(n // tm,),
        in_specs=[tile, tile, _row_spec(gt, tm, tpg, d), gspec,
                  _row_spec(sc, tm, tpg, d), _row_spec(sh, tm, tpg, d)],
        out_specs=(tile, tile),
        out_shape=(jax.ShapeDtypeStruct((n, d), F32), jax.ShapeDtypeStruct((n, d), BF16)),
        compiler_params=_params("parallel"),
        name="resid_norm_mod",
    )(x, y, gt, g, sc, sh)


def _resid_call(x, y, gt, tokens_per_group):
    n, d = x.shape
    tm = _token_tile(n, 512)
    tpg = max(tokens_per_group // tm, 1)
    tile = pl.BlockSpec((tm, d), lambda i: (i, 0))
    return pl.pallas_call(
        _resid_kernel,
        grid=(n // tm,),
        in_specs=[tile, tile, _row_spec(gt, tm, tpg, d)],
        out_specs=tile,
        out_shape=jax.ShapeDtypeStruct((n, d), F32),
        compiler_params=_params("parallel"),
        name="resid",
    )(x, y, gt)


MM_TM = 1024
MM_TN = 512


def _glu_kernel(h_ref, wl_ref, wg_ref, o_ref):
    h = h_ref[...]
    lin = jnp.dot(h, wl_ref[...], preferred_element_type=F32)
    gate = jnp.dot(h, wg_ref[...], preferred_element_type=F32)
    o_ref[...] = lin * jax.nn.sigmoid(gate)


def _glu_call(h, w_in, layer):
    n, d = h.shape
    tm, tn = _token_tile(n, MM_TM), MM_TN
    goff = D_CONV // tn
    return pl.pallas_call(
        _glu_kernel,
        grid=(n // tm, D_CONV // tn),
        in_specs=[
            pl.BlockSpec((tm, d), lambda i, j: (i, 0)),
            pl.BlockSpec((None, d, tn), lambda i, j: (layer, 0, j)),
            pl.BlockSpec((None, d, tn), lambda i, j: (layer, 0, j + goff)),
        ],
        out_specs=pl.BlockSpec((tm, tn), lambda i, j: (i, j)),
        out_shape=jax.ShapeDtypeStruct((n, D_CONV), F32),
        compiler_params=_params("parallel", "arbitrary"),
        name="proj_glu",
    )(h, w_in, w_in)


def _head_proj_kernel(h_ref, w_ref, g_ref, *o_refs, norm, scale):
    z = jnp.dot(h_ref[...], w_ref[...], preferred_element_type=F32)
    for c in range(z.shape[1] // HEAD_DIM):
        cs = slice(c * HEAD_DIM, (c + 1) * HEAD_DIM)
        zc = z[:, cs]
        if norm:
            zc = zc * lax.rsqrt(jnp.mean(zc * zc, axis=-1, keepdims=True) + EPS) * g_ref[...]
        for o_ref in o_refs:
            o_ref[:, cs] = (zc * scale).astype(BF16) if o_ref.dtype == BF16 else zc


def _head_proj_call(h, w_in, layer, col0, g, *, norm, scale, out_dtypes):
    n, d = h.shape
    tm, tn = _token_tile(n, MM_TM), MM_TN
    j0 = col0 // tn
    tile = pl.BlockSpec((tm, tn), lambda i, j: (i, j))
    outs = pl.pallas_call(
        functools.partial(_head_proj_kernel, norm=norm, scale=scale),
        grid=(n // tm, D_ATT // tn),
        in_specs=[
            pl.BlockSpec((tm, d), lambda i, j: (i, 0)),
            pl.BlockSpec((None, d, tn), lambda i, j: (layer, 0, j + j0)),
            pl.BlockSpec((1, HEAD_DIM), lambda i, j: (0, 0)),
        ],
        out_specs=tuple(tile for _ in out_dtypes),
        out_shape=tuple(jax.ShapeDtypeStruct((n, D_ATT), dt) for dt in out_dtypes),
        compiler_params=_params("parallel", "arbitrary"),
        name="proj_heads",
    )(h, w_in, g)
    return outs


def _logf_kernel(h_ref, w_ref, b_ref, o_ref):
    z = jnp.dot(h_ref[...], w_ref[...], preferred_element_type=F32) + b_ref[...]
    o_ref[...] = jnp.minimum(z, 0.0) - jnp.log(1.0 + jnp.exp(-jnp.abs(z)))


def _logf_call(h, wf, bf, layer):
    n, d = h.shape
    tm = _token_tile(n, MM_TM)
    return pl.pallas_call(
        _logf_kernel,
        grid=(n // tm,),
        in_specs=[
            pl.BlockSpec((tm, d), lambda i: (i, 0)),
            pl.BlockSpec((None, d, LANES), lambda i: (layer, 0, 0)),
            pl.BlockSpec((None, 1, LANES), lambda i: (layer, 0, 0)),
        ],
        out_specs=pl.BlockSpec((tm, LANES), lambda i: (i, 0)),
        out_shape=jax.ShapeDtypeStruct((n, LANES), F32),
        compiler_params=_params("parallel"),
        name="proj_logf",
    )(h, wf, bf)


_HALO = 32


def _conv_kernel(a_ref, ap_ref, hist_ref, w_ref, g_ref, b_ref, o_ref, newc_ref, ext_ref, *, tt, n_tiles):
    i = pl.program_id(1)
    nh = CONV_WIDTH - 1
    ext_ref[0:_HALO - nh, :] = jnp.zeros((_HALO - nh, D_CONV), F32)
    ext_ref[_HALO:_HALO + tt, :] = a_ref[...]
    if n_tiles == 1:
        ext_ref[_HALO - nh:_HALO, :] = hist_ref[...]
    else:
        @pl.when(i == 0)
        def _():
            ext_ref[_HALO - nh:_HALO, :] = hist_ref[...]

        @pl.when(i > 0)
        def _():
            ext_ref[_HALO - nh:_HALO, :] = ap_ref[tt - nh:tt, :]

    cols = []
    for c in range(D_CONV // LANES):
        c0 = c * LANES
        win = ext_ref[:, c0:c0 + LANES]
        acc = jnp.zeros((tt, LANES), F32)
        for s in range(SUBLANES):
            sh = win if s == 0 else win[s:s + tt + _HALO - SUBLANES, :]
            for q in range(_HALO // SUBLANES + 1):
                j = SUBLANES * q + s - (_HALO - nh)
                if 0 <= j < CONV_WIDTH:
                    acc = acc + sh[SUBLANES * q:SUBLANES * q + tt, :] * w_ref[j:j + 1, c0:c0 + LANES]
        cols.append(acc)
    y = jnp.concatenate(cols, axis=1)
    mu = jnp.mean(y, axis=-1, keepdims=True)
    yc = y - mu
    z = yc * lax.rsqrt(jnp.mean(yc * yc, axis=-1, keepdims=True) + EPS) * g_ref[...] + b_ref[...]
    o_ref[...] = (z * jax.nn.sigmoid(z)).astype(o_ref.dtype)

    @pl.when(i == n_tiles - 1)
    def _():
        newc_ref[...] = ext_ref[tt + _HALO - nh:tt + _HALO, :]


def _conv_call(a, hist, conv_w, ln_g, ln_b):
    b, t, dc = a.shape
    tt = 256 if t % 256 == 0 else t
    n_tiles = t // tt
    nh = CONV_WIDTH - 1
    return pl.pallas_call(
        functools.partial(_conv_kernel, tt=tt, n_tiles=n_tiles),
        grid=(b, n_tiles),
        in_specs=[
            pl.BlockSpec((None, tt, dc), lambda bi, i: (bi, i, 0)),
            pl.BlockSpec((None, tt, dc), lambda bi, i: (bi, jnp.maximum(i - 1, 0), 0)),
            pl.BlockSpec((None, nh, dc), lambda bi, i: (bi, 0, 0)),
            pl.BlockSpec((CONV_WIDTH, dc), lambda bi, i: (0, 0)),
            pl.BlockSpec((1, dc), lambda bi, i: (0, 0)),
            pl.BlockSpec((1, dc), lambda bi, i: (0, 0)),
        ],
        out_specs=(
            pl.BlockSpec((None, tt, dc), lambda bi, i: (bi, i, 0)),
            pl.BlockSpec((None, nh, dc), lambda bi, i: (bi, 0, 0)),
        ),
        out_shape=(jax.ShapeDtypeStruct((b, t, dc), BF16), jax.ShapeDtypeStruct((b, nh, dc), F32)),
        scratch_shapes=[pltpu.VMEM((tt + _HALO, dc), F32)],
        compiler_params=_params("parallel", "arbitrary"),
        name="conv_branch",
    )(a, a, hist, conv_w, ln_g, ln_b)


_CS_BLK = 256


def _cumsum_kernel(x_ref, o_ref):
    t = x_ref.shape[0]
    r = lax.broadcasted_iota(jnp.int32, (_CS_BLK, _CS_BLK), 0)
    c = lax.broadcasted_iota(jnp.int32, (_CS_BLK, _CS_BLK), 1)
    tri = (c <= r).astype(F32)
    carry = jnp.zeros((1, x_ref.shape[1]), F32)
    for blk in range(t // _CS_BLK):
        rows = slice(blk * _CS_BLK, (blk + 1) * _CS_BLK)
        f = jnp.dot(tri, x_ref[rows, :], precision=lax.Precision.HIGHEST, preferred_element_type=F32) + carry
        o_ref[rows, :] = f
        carry = f[_CS_BLK - 1:_CS_BLK, :]


def _cumsum_call(x):
    b, t, w = x.shape
    return pl.pallas_call(
        _cumsum_kernel,
        grid=(b,),
        in_specs=[pl.BlockSpec((None, t, w), lambda bi: (bi, 0, 0))],
        out_specs=pl.BlockSpec((None, t, w), lambda bi: (bi, 0, 0)),
        out_shape=jax.ShapeDtypeStruct((b, t, w), F32),
        compiler_params=_params("parallel"),
        name="logf_cumsum",
    )(x)


ATT_TQ = 512


def _attn_kernel(q_ref, k_ref, v_ref, fq_ref, fk_ref, o_ref, *, tq):
    i = pl.program_id(2)
    q = q_ref[...]
    fq = fq_ref[...]
    qpos = i * tq + lax.broadcasted_iota(jnp.int32, (tq, tq), 0)
    kidx = lax.broadcasted_iota(jnp.int32, (tq, tq), 1)

    def body(j, carry):
        m, l, acc = carry
        ks = pl.multiple_of(j * tq, tq)
        k = k_ref[pl.ds(ks, tq), :]
        v = v_ref[pl.ds(ks, tq), :]
        s = lax.dot_general(q, k, _NT, preferred_element_type=F32)
        s = s + fq - fk_ref[:, pl.ds(ks, tq)]
        s = jnp.where(ks + kidx <= qpos, s, -jnp.inf)
        m_new = jnp.maximum(m, jnp.max(s, axis=1, keepdims=True))
        p = jnp.exp(s - m_new)
        alpha = jnp.exp(m - m_new)
        l = alpha * l + jnp.sum(p, axis=1, keepdims=True)
        acc = alpha * acc + jnp.dot(p.astype(BF16), v, preferred_element_type=F32)
        return m_new, l, acc

    init = (jnp.full((tq, 1), -jnp.inf, F32), jnp.zeros((tq, 1), F32), jnp.zeros((tq, HEAD_DIM), F32))
    _, l, acc = lax.fori_loop(0, i + 1, body, init)
    o_ref[...] = (acc / l).astype(o_ref.dtype)


def _attn_prompt_call(q, k, v, f_col, f_row, batch, t):
    tq = ATT_TQ
    nq = t // tq
    return pl.pallas_call(
        functools.partial(_attn_kernel, tq=tq),
        grid=(batch, N_HEADS, nq),
        in_specs=[
            pl.BlockSpec((tq, HEAD_DIM), lambda b, h, i: (b * nq + i, h)),
            pl.BlockSpec((t, HEAD_DIM), lambda b, h, i: (b, h)),
            pl.BlockSpec((t, HEAD_DIM), lambda b, h, i: (b, h)),
            pl.BlockSpec((None, None, tq, 1), lambda b, h, i: (b, h, i, 0)),
            pl.BlockSpec((None, None, 1, t), lambda b, h, i: (b, h, 0, 0)),
        ],
        out_specs=pl.BlockSpec((tq, HEAD_DIM), lambda b, h, i: (b * nq + i, h)),
        out_shape=jax.ShapeDtypeStruct((batch * t, D_ATT), BF16),
        compiler_params=_params("parallel", "parallel", "arbitrary"),
        name="fox_prompt",
    )(q, k, v, f_col, f_row)


ATT_PAGES = 8


def _sample_attn_kernel(pt_ref, qt_ref, kn_ref, vn_ref, lfn_ref, *rest, n_new, n_steps):
    kp = rest[0:ATT_PAGES]
    vp = rest[ATT_PAGES:2 * ATT_PAGES]
    lp = rest[2 * ATT_PAGES:3 * ATT_PAGES]
    o_ref, m_ref, l_ref, acc_ref, carry_ref, fq_ref, mask_ref, tri_ref = rest[3 * ATT_PAGES:]
    g = pl.program_id(1)
    cols = qt_ref.shape[1]
    page = kp[0].shape[0]
    rows = page * N_HEADS
    new_rows = n_new * N_HEADS

    nr = lax.broadcasted_iota(jnp.int32, (new_rows, cols), 0)
    nc = lax.broadcasted_iota(jnp.int32, (new_rows, cols), 1)
    new_s, new_h = nr // N_HEADS, nr % N_HEADS
    col_t, col_h = nc // N_HEADS, nc % N_HEADS

    @pl.when(g == 0)
    def _():
        fs = jnp.zeros((new_rows, cols), F32)
        for t in range(n_new):
            fs = fs + jnp.where(new_s >= t, lfn_ref[t:t + 1, :], 0.0)
        fq_ref[0:new_rows, :] = fs
        pick = jnp.where(new_s == col_t, jnp.where(new_h == 0, fs, 0.0), 0.0)
        fq_ref[new_rows:new_rows + 1, :] = jnp.sum(pick, axis=0, keepdims=True)
        m_ref[...] = jnp.full(m_ref.shape, -jnp.inf, F32)
        l_ref[...] = jnp.zeros(l_ref.shape, F32)
        acc_ref[...] = jnp.zeros(acc_ref.shape, F32)
        carry_ref[...] = jnp.zeros(carry_ref.shape, F32)
        r = lax.broadcasted_iota(jnp.int32, (rows, cols), 0)
        c = lax.broadcasted_iota(jnp.int32, (rows, cols), 1)
        mask_ref[...] = jnp.where(r % N_HEADS == c % N_HEADS, 0.0, -jnp.inf)
        tr = lax.broadcasted_iota(jnp.int32, (rows, page), 0) // N_HEADS
        tc = lax.broadcasted_iota(jnp.int32, (rows, page), 1)
        tri_ref[...] = jnp.where(tc > tr, 1.0, 0.0).astype(BF16)

    qt = qt_ref[...]
    fq = fq_ref[new_rows:new_rows + 1, :]

    def update(scores, values):
        m_old = m_ref[...]
        m_new = m_old
        for s in scores:
            m_new = jnp.maximum(m_new, jnp.max(s, axis=0, keepdims=True))
        alpha = jnp.exp(m_old - m_new)
        l_new = alpha * l_ref[...]
        acc = alpha * acc_ref[...]
        for s, v_bf in zip(scores, values):
            p = jnp.exp(s - m_new)
            l_new = l_new + jnp.sum(p, axis=0, keepdims=True)
            acc = acc + lax.dot_general(v_bf, p.astype(BF16), _TN, preferred_element_type=F32)
        l_ref[...] = l_new
        acc_ref[...] = acc
        m_ref[...] = m_new

    tri = tri_ref[...]
    carry = carry_ref[...]
    scores, values = [], []
    for r in range(ATT_PAGES):
        lpv = lp[r][...]
        hi = lpv.astype(BF16)
        rem = lpv - hi.astype(F32)
        mid = rem.astype(BF16)
        lo = (rem - mid.astype(F32)).astype(BF16)
        after = (jnp.dot(tri, hi, preferred_element_type=F32) + jnp.dot(tri, mid, preferred_element_type=F32)
                 + jnp.dot(tri, lo, preferred_element_type=F32))
        k2 = kp[r][...].reshape(rows, HEAD_DIM).astype(BF16)
        s = jnp.dot(k2, qt, preferred_element_type=F32)
        scores.append(s + after + (carry + fq) + mask_ref[...])
        values.append(vp[r][...].reshape(rows, HEAD_DIM).astype(BF16))
        carry = carry + after[0:1, :] + lpv[0:1, :]
    update(scores, values)
    carry_ref[...] = carry

    @pl.when(g == n_steps - 1)
    def _():
        s = jnp.dot(kn_ref[...], qt, preferred_element_type=F32)
        s = s + fq - fq_ref[0:new_rows, :]
        s = jnp.where(new_h == col_h, jnp.where(new_s <= col_t, s, -jnp.inf), -jnp.inf)
        update([s], [vn_ref[...]])
        o_ref[...] = acc_ref[...] / l_ref[...]


def _sample_attn_call(page_table, qt, k_new, v_new, lf_new, cache_k, cache_v, cache_lf, layer, n_new):
    bd, n_pages = page_table.shape
    page = cache_k.shape[2]
    cols = qt.shape[2]
    new_rows = n_new * N_HEADS
    n_steps = n_pages // ATT_PAGES
    lf_rows = lf_new.shape[1]

    def kv_map(r):
        return lambda b, g, pt: (layer, pt[b, n_pages - 1 - (g * ATT_PAGES + r)], 0, 0, 0)

    def lf_map(r):
        return lambda b, g, pt: (layer, pt[b, n_pages - 1 - (g * ATT_PAGES + r)], 0, 0)

    per_b = lambda b, g, pt: (b, 0, 0)
    in_specs = [
        pl.BlockSpec((None, HEAD_DIM, cols), per_b),
        pl.BlockSpec((None, new_rows, HEAD_DIM), per_b),
        pl.BlockSpec((None, new_rows, HEAD_DIM), per_b),
        pl.BlockSpec((None, lf_rows, cols), per_b),
    ]
    in_specs += [pl.BlockSpec((None, None, page, N_HEADS, HEAD_DIM), kv_map(r)) for r in range(ATT_PAGES)]
    in_specs += [pl.BlockSpec((None, None, page, N_HEADS, HEAD_DIM), kv_map(r)) for r in range(ATT_PAGES)]
    in_specs += [pl.BlockSpec((None, None, page, cols), lf_map(r)) for r in range(ATT_PAGES)]
    grid_spec = pltpu.PrefetchScalarGridSpec(
        num_scalar_prefetch=1,
        grid=(bd, n_steps),
        in_specs=in_specs,
        out_specs=pl.BlockSpec((None, HEAD_DIM, cols), per_b),
        scratch_shapes=[
            pltpu.VMEM((1, cols), F32),
            pltpu.VMEM((1, cols), F32),
            pltpu.VMEM((HEAD_DIM, cols), F32),
            pltpu.VMEM((1, cols), F32),
            pltpu.VMEM((new_rows + SUBLANES, cols), F32),
            pltpu.VMEM((page * N_HEADS, cols), F32),
            pltpu.VMEM((page * N_HEADS, page), BF16),
        ],
    )
    return pl.pallas_call(
        functools.partial(_sample_attn_kernel, n_new=n_new, n_steps=n_steps),
        grid_spec=grid_spec,
        out_shape=jax.ShapeDtypeStruct((bd, HEAD_DIM, cols), F32),
        compiler_params=_params("parallel", "arbitrary"),
        name="fox_sample",
    )(page_table, qt, k_new, v_new, lf_new,
      *([cache_k] * ATT_PAGES), *([cache_v] * ATT_PAGES), *([cache_lf] * ATT_PAGES))


def _gated_kernel(h_ref, wg_ref, a_ref, w_ref, *rest):
    gate = jax.nn.sigmoid(jnp.dot(h_ref[...], wg_ref[...], preferred_element_type=F32))
    y = gate * jnp.dot(a_ref[...], w_ref[...], preferred_element_type=F32)
    if len(rest) == 2:
        y = y + rest[0][...]
    o_ref = rest[-1]
    o_ref[...] = y.astype(o_ref.dtype)


def _gated_call(h, w_gates, layer, gate_col0, a, w, prev, out_dtype):
    n, d = h.shape
    ka = a.shape[1]
    tm, tn = _token_tile(n, MM_TM), MM_TN
    j0 = gate_col0 // tn
    tile = pl.BlockSpec((tm, tn), lambda i, j: (i, j))
    in_specs = [
        pl.BlockSpec((tm, d), lambda i, j: (i, 0)),
        pl.BlockSpec((None, d, tn), lambda i, j: (layer, 0, j + j0)),
        pl.BlockSpec((tm, ka), lambda i, j: (i, 0)),
        pl.BlockSpec((None, ka, tn), lambda i, j: (layer, 0, j)),
    ]
    args = [h, w_gates, a, w]
    if prev is not None:
        in_specs.append(tile)
        args.append(prev)
    return pl.pallas_call(
        _gated_kernel,
        grid=(n // tm, D_MODEL // tn),
        in_specs=in_specs,
        out_specs=tile,
        out_shape=jax.ShapeDtypeStruct((n, D_MODEL), out_dtype),
        compiler_params=_params("parallel", "arbitrary"),
        name="gated_branch",
    )(*args)


def _out_proj_kernel(m_ref, w_ref, x_ref, gt_ref, o_ref):
    o_ref[...] = x_ref[...] + gt_ref[...] * jnp.dot(m_ref[...], w_ref[...], preferred_element_type=F32)


def _out_proj_call(m, w_out, layer, x, gt, tokens_per_group):
    n, d = m.shape
    tm, tn = _token_tile(n, MM_TM), MM_TN
    tpg = max(tokens_per_group // tm, 1)
    tile = pl.BlockSpec((tm, tn), lambda i, j: (i, j))
    return pl.pallas_call(
        _out_proj_kernel,
        grid=(n // tm, D_MODEL // tn),
        in_specs=[
            pl.BlockSpec((tm, d), lambda i, j: (i, 0)),
            pl.BlockSpec((None, d, tn), lambda i, j: (layer, 0, j)),
            tile,
            _row_spec(gt, tm, tpg, tn, col=True),
        ],
        out_specs=tile,
        out_shape=jax.ShapeDtypeStruct((n, D_MODEL), F32),
        compiler_params=_params("parallel", "arbitrary"),
        name="out_proj",
    )(m, w_out, x, gt)


def _peer_score_kernel(h_ref, w_ref, sk_ref, o_ref):
    z = jnp.dot(h_ref[...], w_ref[...], preferred_element_type=F32).astype(BF16)
    for p in range(2):
        zp = z[:, p * N_KEYS:(p + 1) * N_KEYS]
        o_ref[p] = lax.dot_general(sk_ref[p], zp, _NT, preferred_element_type=F32)


def _peer_score_call(h, w_pq, sub_keys, layer):
    n, d = h.shape
    tm = _token_tile(n, MM_TM)
    dk = 2 * N_KEYS
    return pl.pallas_call(
        _peer_score_kernel,
        grid=(n // tm, PEER_HEADS),
        in_specs=[
            pl.BlockSpec((tm, d), lambda i, j: (i, 0)),
            pl.BlockSpec((None, d, dk), lambda i, j: (layer, 0, j)),
            pl.BlockSpec((None, 2, N_KEYS, N_KEYS), lambda i, j: (layer, j, 0, 0)),
        ],
        out_specs=pl.BlockSpec((2, N_KEYS, tm), lambda i, j: (j, 0, i)),
        out_shape=jax.ShapeDtypeStruct((2 * PEER_HEADS, N_KEYS, n), F32),
        compiler_params=_params("parallel", "arbitrary"),
        name="peer_scores",
    )(h, w_pq, sub_keys)


def _knock_out_max(s, rid):
    mx = jnp.max(s, axis=0, keepdims=True)
    pos = jnp.min(jnp.where(s == mx, rid, s.shape[0]), axis=0, keepdims=True)
    hit = rid == pos
    return mx, jnp.where(hit, -jnp.inf, s), hit


def _top_desc(s, k, want_rank):
    rid = lax.broadcasted_iota(jnp.int32, s.shape, 0)
    rk = lax.broadcasted_iota(jnp.int32, (k, s.shape[1]), 0)
    tops = jnp.full((k, s.shape[1]), -jnp.inf, F32)
    rank = jnp.full(s.shape, float(k), F32)
    for r in range(k):
        mx, s, hit = _knock_out_max(s, rid)
        tops = jnp.where(rk == r, mx, tops)
        if want_rank:
            rank = jnp.where(hit, float(r), rank)
    return tops, rank


def _peer_topk_kernel(st_ref, rank_ref, p2_ref, cnt_ref, c1_ref):
    cw = st_ref.shape[2]
    k = PEER_TOPK
    row8 = lax.broadcasted_iota(jnp.int32, (SUBLANES, cw), 0)

    def one_head(h):
        s1 = st_ref[2 * h]
        s2 = st_ref[2 * h + 1]
        t1, _ = _top_desc(s1, k, False)
        t2, rank2 = _top_desc(s2, k, True)
        blocks = [t2 + t1[0:1, :], t2[0:SUBLANES, :] + t1[1:2, :]]
        for a in range(2, SUBLANES):
            blocks.append(jnp.where(row8 < k // (a + 1), t2[0:SUBLANES, :] + t1[a:a + 1, :], -jnp.inf))
        blocks.append(t1[SUBLANES:k, :] + t2[0:1, :])
        cand = jnp.concatenate(blocks, axis=0)
        rid = lax.broadcasted_iota(jnp.int32, cand.shape, 0)
        work = cand
        for _ in range(k):
            tau, work, _ = _knock_out_max(work, rid)
        m1 = t1[0:1, :]
        m2 = t2[0:1, :]
        z = jnp.sum(jnp.where(cand >= tau, jnp.exp(cand - (m1 + m2)), 0.0), axis=0, keepdims=True)
        cnt = jnp.zeros(s1.shape, F32)
        for b in range(k):
            cnt = cnt + jnp.where(s1 + t2[b:b + 1, :] >= tau, 1.0, 0.0)
        rank_ref[h] = rank2.astype(rank_ref.dtype)
        p2_ref[h] = jnp.exp(s2 - m2).astype(p2_ref.dtype)
        cnt_ref[h] = cnt
        c1_ref[h] = jnp.exp(s1 - m1) / z

    def head_pair(hp, carry):
        one_head(2 * hp)
        one_head(2 * hp + 1)
        return carry

    lax.fori_loop(0, PEER_HEADS // 2, head_pair, 0)


def _peer_topk_call(st):
    n = st.shape[2]
    cw = LANES if n % LANES == 0 else n
    spec = pl.BlockSpec((PEER_HEADS, N_KEYS, cw), lambda i: (0, 0, i))
    shape = lambda dt: jax.ShapeDtypeStruct((PEER_HEADS, N_KEYS, n), dt)
    return pl.pallas_call(
        _peer_topk_kernel,
        grid=(n // cw,),
        in_specs=[pl.BlockSpec((2 * PEER_HEADS, N_KEYS, cw), lambda i: (0, 0, i))],
        out_specs=(spec, spec, spec, spec),
        out_shape=(shape(BF16), shape(BF16), shape(F32), shape(F32)),
        compiler_params=_params("parallel"),
        name="peer_topk",
    )(st)


PEER_TM = 512
PEER_E1 = 8
PACK = 16

_INV_SQRT2 = 0.7071067811865476


def _peer_mix_kernel(h_ref, u_ref, v_ref, rank_ref, p2_ref, cnt_ref, c1_ref, o_ref, act_ref, wa_ref, wb_ref,
                     *, cw, n_blk):
    k = pl.program_id(1)
    kc = jnp.minimum(k, n_blk - 1)
    tm = h_ref.shape[0]

    @pl.when(k == 0)
    def _():
        wb_ref[...] = jnp.zeros(wb_ref.shape, BF16)
        o_ref[...] = jnp.zeros(o_ref.shape, F32)

    def step(prev_ref, cur_ref):
        o_ref[...] += lax.dot_general(prev_ref[...], v_ref[...], _TN, preferred_element_type=F32)
        act_ref[...] = lax.dot_general(u_ref[...], h_ref[...], _NT, preferred_element_type=F32)
        zero = jnp.zeros((), BF16)
        for g in range(PEER_E1):
            e1 = kc * PEER_E1 + g
            for ch in range(tm // cw):
                cs = slice(ch * cw, (ch + 1) * cw)
                cnt8 = cnt_ref[e1, :, cs]
                c18 = c1_ref[e1, :, cs]
                cnt_b = [jnp.broadcast_to(cnt8[h:h + 1, :], (PACK, cw)).astype(BF16) for h in range(PEER_HEADS)]
                c1_b = [jnp.broadcast_to(c18[h:h + 1, :], (PACK, cw)).astype(BF16) for h in range(PEER_HEADS)]
                for rg in range(N_KEYS // PACK):
                    rs = slice(rg * PACK, (rg + 1) * PACK)
                    w = jnp.zeros((PACK, cw), BF16)
                    for h in range(PEER_HEADS):
                        w = w + jnp.where(rank_ref[h, rs, cs] < cnt_b[h], p2_ref[h, rs, cs] * c1_b[h], zero)
                    ers = slice(g * N_KEYS + rg * PACK, g * N_KEYS + (rg + 1) * PACK)
                    a = act_ref[ers, cs]
                    gelu = 0.5 * a * (1.0 + lax.erf(a * _INV_SQRT2))
                    cur_ref[ers, cs] = w * gelu.astype(BF16)

    @pl.when(k % 2 == 0)
    def _():
        step(wb_ref, wa_ref)

    @pl.when(k % 2 == 1)
    def _():
        step(wa_ref, wb_ref)


def _peer_mix_call(h, u, v, layer, rank2, p2, cnt, c1):
    n, d = h.shape
    n_exp = u.shape[1]
    tm = _token_tile(n, PEER_TM)
    cw = LANES if tm % LANES == 0 else tm
    eb = PEER_E1 * N_KEYS
    n_blk = n_exp // eb
    table = pl.BlockSpec((PEER_HEADS, N_KEYS, tm), lambda i, k: (0, 0, i))
    table_e1 = pl.BlockSpec((N_KEYS, PEER_HEADS, tm), lambda i, k: (0, 0, i))
    return pl.pallas_call(
        functools.partial(_peer_mix_kernel, cw=cw, n_blk=n_blk),
        grid=(n // tm, n_blk + 1),
        in_specs=[
            pl.BlockSpec((tm, d), lambda i, k: (i, 0)),
            pl.BlockSpec((None, eb, d), lambda i, k: (layer, jnp.minimum(k, n_blk - 1), 0)),
            pl.BlockSpec((None, eb, d), lambda i, k: (layer, jnp.maximum(k - 1, 0), 0)),
            table, table, table_e1, table_e1,
        ],
        out_specs=pl.BlockSpec((tm, d), lambda i, k: (i, 0)),
        out_shape=jax.ShapeDtypeStruct((n, d), F32),
        scratch_shapes=[pltpu.VMEM((eb, tm), F32), pltpu.VMEM((eb, tm), BF16), pltpu.VMEM((eb, tm), BF16)],
        compiler_params=_params("parallel", "arbitrary"),
        name="peer_mix",
    )(h, u, v, rank2, p2, cnt, c1)


def _layer(x, pending, mods, hist, lw, sw, layer, batch, t, attend):
    sh1, sc1, gt1, sh2, sc2, gt2 = mods
    if pending is None:
        _, h1 = _norm_call(x, lw["g_norm1"], sc1, sh1, t)
    else:
        x, h1 = _norm_call(x, lw["g_norm1"], sc1, sh1, t, pending)
    a = _glu_call(h1, sw["w_in"], layer)
    scale = HEAD_DIM ** -0.5
    (q_bf,) = _head_proj_call(h1, sw["w_in"], layer, 2 * D_CONV, lw["g_q"], norm=True, scale=scale,
                              out_dtypes=(BF16,))
    k_f32, k_bf = _head_proj_call(h1, sw["w_in"], layer, 2 * D_CONV + D_ATT, lw["g_k"], norm=True, scale=1.0,
                                  out_dtypes=(F32, BF16))
    v_f32, v_bf = _head_proj_call(h1, sw["w_in"], layer, 2 * D_CONV + 2 * D_ATT, lw["g_k"], norm=False, scale=1.0,
                                  out_dtypes=(F32, BF16))
    logf = _logf_call(h1, sw["w_f"], sw["b_f"], layer)
    act, new_conv = _conv_call(a.reshape(batch, t, D_CONV), hist, lw["conv_w"], lw["conv_ln_g"], lw["conv_ln_b"])
    o = attend(q_bf, k_bf, v_bf, logf)
    part = _gated_call(h1, sw["w_gates"], layer, 0, act.reshape(batch * t, D_CONV), sw["w_a_proj"], None, F32)
    mix = _gated_call(h1, sw["w_gates"], layer, D_MODEL, o, sw["w_b_proj"], part, BF16)
    x = _out_proj_call(mix, sw["w_out"], layer, x, gt1, t)
    _, h2 = _norm_call(x, lw["g_norm2"], sc2, sh2, t)
    st = _peer_score_call(h2, sw["w_pq"], sw["sub_keys"], layer)
    rank2, p2, cnt, c1 = _peer_topk_call(st)
    y = _peer_mix_call(h2, sw["peer_u"], sw["peer_v"], layer, rank2, p2,
                       cnt.transpose(1, 0, 2), c1.transpose(1, 0, 2))
    return x, (y, gt2), k_f32, v_f32, logf[:, :N_HEADS], new_conv


def kernel(x_prompt, x_sample, c_prompt, c_sample, cache_k, cache_v, cache_logf, state_conv, page_table, w_ada, b_ada, g_norm1, g_norm2, w_in, b_f, g_q, g_k, conv_w, conv_ln_g, conv_ln_b, w_a_proj, w_b_proj, w_out, w_pq, sub_keys, peer_u, peer_v):
    bp, seq, d = x_prompt.shape
    bd, tdec, _ = x_sample.shape
    depth = w_ada.shape[0]
    cols = tdec * N_HEADS

    rows = bp + bd
    rows_pad = -(-rows // SUBLANES) * SUBLANES
    c_all = jnp.concatenate([c_prompt, c_sample, jnp.zeros((rows_pad - rows, d), F32)], axis=0)
    mod = _ada_call(c_all, w_ada, b_ada)

    c_f = 2 * D_CONV + 3 * D_ATT
    w_in_bf = w_in.astype(BF16)
    w_gates = w_in[:, :, c_f + N_HEADS:].astype(BF16)
    w_f = jnp.pad(w_in[:, :, c_f:c_f + N_HEADS], ((0, 0), (0, 0), (0, LANES - N_HEADS))).astype(BF16)
    b_f_pad = jnp.pad(b_f, ((0, 0), (0, LANES - N_HEADS))).reshape(depth, 1, LANES)
    w_a_bf, w_b_bf, w_out_bf, w_pq_bf = (w.astype(BF16) for w in (w_a_proj, w_b_proj, w_out, w_pq))
    sk_bf = sub_keys.reshape(depth, 2 * PEER_HEADS, N_KEYS, N_KEYS).astype(BF16)
    sw = dict(w_in=w_in_bf, w_gates=w_gates, w_f=w_f, b_f=b_f_pad, w_a_proj=w_a_bf, w_b_proj=w_b_bf,
              w_out=w_out_bf, w_pq=w_pq_bf, sub_keys=sk_bf, peer_u=peer_u.astype(BF16), peer_v=peer_v.astype(BF16))

    clf = jnp.tile(cache_logf, (1, 1, 1, tdec))

    def attend_prompt(q_bf, k_bf, v_bf, logf):
        f = _cumsum_call(logf.reshape(bp, seq, LANES))[:, :, :N_HEADS]
        f_t = f.transpose(0, 2, 1)
        return _attn_prompt_call(q_bf, k_bf, v_bf, f_t[..., None], f_t[:, :, None, :], bp, seq)

    def make_attend_sample(l):
        def attend_sample(q_bf, k_bf, v_bf, logf):
            qt = q_bf.reshape(bd, cols, HEAD_DIM).transpose(0, 2, 1)
            kn = k_bf.reshape(bd, cols, HEAD_DIM)
            vn = v_bf.reshape(bd, cols, HEAD_DIM)
            lfn = jnp.tile(logf[:, :N_HEADS].reshape(bd, tdec, N_HEADS), (1, 1, tdec))
            lfn = jnp.pad(lfn, ((0, 0), (0, SUBLANES - tdec), (0, 0)))
            o_t = _sample_attn_call(page_table, qt, kn, vn, lfn, cache_k, cache_v, clf, l, tdec)
            return o_t.transpose(0, 2, 1).reshape(bd * tdec, D_ATT).astype(BF16)
        return attend_sample

    xp = x_prompt.reshape(bp * seq, d)
    xs = x_sample.reshape(bd * tdec, d)
    pend_p = pend_s = None
    zero_hist = jnp.zeros((bp, CONV_WIDTH - 1, D_CONV), F32)
    outs_p, outs_s = [], []
    for l in range(depth):
        lw = dict(g_norm1=g_norm1[l][None], g_norm2=g_norm2[l][None], g_q=g_q[l][None], g_k=g_k[l][None],
                  conv_w=conv_w[l], conv_ln_g=conv_ln_g[l][None], conv_ln_b=conv_ln_b[l][None])
        mods_p = tuple(m[:, None, :] for m in jnp.split(mod[l, :bp], 6, axis=-1))
        mods_s = tuple(jnp.repeat(m, tdec, axis=0)[None] for m in jnp.split(mod[l, bp:rows], 6, axis=-1))
        xp, pend_p, kp, vp, fp, cp = _layer(xp, pend_p, mods_p, zero_hist, lw, sw, l, bp, seq, attend_prompt)
        xs, pend_s, ks, vs, fs, cs = _layer(xs, pend_s, mods_s, state_conv[l], lw, sw, l, bd, tdec, make_attend_sample(l))
        outs_p.append((kp.reshape(bp, seq, N_HEADS, HEAD_DIM), vp.reshape(bp, seq, N_HEADS, HEAD_DIM),
                       fp.reshape(bp, seq, N_HEADS), cp))
        outs_s.append((ks.reshape(bd, tdec, N_HEADS, HEAD_DIM), vs.reshape(bd, tdec, N_HEADS, HEAD_DIM),
                       fs.reshape(bd, tdec, N_HEADS), cs))
    xp = _resid_call(xp, pend_p[0], pend_p[1], seq).reshape(bp, seq, d)
    xs = _resid_call(xs, pend_s[0], pend_s[1], tdec).reshape(bd, tdec, d)
    stack = lambda outs, i: jnp.stack([o[i] for o in outs])
    return (xp, xs, stack(outs_p, 0), stack(outs_p, 1), stack(outs_p, 2), stack(outs_p, 3),
            stack(outs_s, 0), stack(outs_s, 1), stack(outs_s, 2), stack(outs_s, 3))
```

```python
import functools

import jax
import jax.numpy as jnp
from jax import lax
from jax.experimental import pallas as pl
from jax.experimental.pallas import tpu as pltpu

F32 = jnp.float32
BF16 = jnp.bfloat16

D_MODEL = 2048
D_CONV = 1024
CONV_WIDTH = 31
N_HEADS = 8
HEAD_DIM = 128
D_ATT = N_HEADS * HEAD_DIM
PEER_HEADS = 8
N_KEYS = 128
PEER_TOPK = 16
EPS = 1e-6

LANES = 128
SUBLANES = 8
VMEM_LIMIT = 56 * 1024 * 1024

_NT = (((1,), (1,)), ((), ()))
_TN = (((0,), (0,)), ((), ()))


def _params(*sem):
    return pltpu.CompilerParams(dimension_semantics=sem, vmem_limit_bytes=VMEM_LIMIT)


def _row_spec(arr, tm, tiles_per_group, width, col=None):
    r = arr.shape[1]
    if col is None:
        return pl.BlockSpec((None, r, width), lambda i, *_: (i // tiles_per_group, 0, 0))
    return pl.BlockSpec((None, r, width), lambda i, j: (i // tiles_per_group, 0, j))


def _ada_kernel(c_ref, w_ref, b_ref, o_ref):
    c = c_ref[...]
    s = (c * jax.nn.sigmoid(c)).astype(BF16)
    o_ref[...] = jnp.dot(s, w_ref[...].astype(BF16), preferred_element_type=F32) + b_ref[...]


def _ada_call(c, w_ada, b_ada):
    depth, d, n6 = w_ada.shape
    rows = c.shape[0]
    tn = 1024
    return pl.pallas_call(
        _ada_kernel,
        grid=(depth, n6 // tn),
        in_specs=[
            pl.BlockSpec((rows, d), lambda l, j: (0, 0)),
            pl.BlockSpec((None, d, tn), lambda l, j: (l, 0, j)),
            pl.BlockSpec((None, 1, tn), lambda l, j: (l, 0, j)),
        ],
        out_specs=pl.BlockSpec((None, rows, tn), lambda l, j: (l, 0, j)),
        out_shape=jax.ShapeDtypeStruct((depth, rows, n6), F32),
        compiler_params=_params("parallel", "parallel"),
        name="ada_mod",
    )(c, w_ada, b_ada.reshape(depth, 1, n6))


def _modulated(x, g, sc, sh):
    y = x * lax.rsqrt(jnp.mean(x * x, axis=-1, keepdims=True) + EPS) * g
    return y * (1.0 + sc) + sh


def _norm_kernel(x_ref, g_ref, sc_ref, sh_ref, h_ref):
    h_ref[...] = _modulated(x_ref[...], g_ref[...], sc_ref[...], sh_ref[...]).astype(h_ref.dtype)


def _resid_norm_kernel(x_ref, y_ref, gt_ref, g_ref, sc_ref, sh_ref, xn_ref, h_ref):
    x = x_ref[...] + gt_ref[...] * y_ref[...]
    xn_ref[...] = x
    h_ref[...] = _modulated(x, g_ref[...], sc_ref[...], sh_ref[...]).astype(h_ref.dtype)


def _resid_kernel(x_ref, y_ref, gt_ref, xn_ref):
    xn_ref[...] = x_ref[...] + gt_ref[...] * y_ref[...]


def _token_tile(n, pref):
    return pref if n % pref == 0 else n


def _norm_call(x, g, sc, sh, tokens_per_group, pending=None):
    n, d = x.shape
    tm = _token_tile(n, 512)
    tpg = max(tokens_per_group // tm, 1)
    tile = pl.BlockSpec((tm, d), lambda i: (i, 0))
    gspec = pl.BlockSpec((1, d), lambda i: (0, 0))
    if pending is None:
        return None, pl.pallas_call(
            _norm_kernel,
            grid=(n // tm,),
            in_specs=[tile, gspec, _row_spec(sc, tm, tpg, d), _row_spec(sh, tm, tpg, d)],
            out_specs=tile,
            out_shape=jax.ShapeDtypeStruct((n, d), BF16),
            compiler_params=_params("parallel"),
            name="norm_mod",
        )(x, g, sc, sh)
    y, gt = pending
    return pl.pallas_call(
        _resid_norm_kernel,
        grid=(n // tm,),
        in_specs=[tile, tile, _row_spec(gt, tm, tpg, d), gspec,
                  _row_spec(sc, tm, tpg, d), _row_spec(sh, tm, tpg, d)],
        out_specs=(tile, tile),
        out_shape=(jax.ShapeDtypeStruct((n, d), F32), jax.ShapeDtypeStruct((n, d), BF16)),
        compiler_params=_params("parallel"),
        name="resid_norm_mod",
    )(x, y, gt, g, sc, sh)


def _resid_call(x, y, gt, tokens_per_group):
    n, d = x.shape
    tm = _token_tile(n, 512)
    tpg = max(tokens_per_group // tm, 1)
    tile = pl.BlockSpec((tm, d), lambda i: (i, 0))
    return pl.pallas_call(
        _resid_kernel,
        grid=(n // tm,),
        in_specs=[tile, tile, _row_spec(gt, tm, tpg, d)],
        out_specs=tile,
        out_shape=jax.ShapeDtypeStruct((n, d), F32),
        compiler_params=_params("parallel"),
        name="resid",
    )(x, y, gt)


MM_TM = 1024
MM_TN = 512


def _glu_kernel(h_ref, wl_ref, wg_ref, o_ref):
    h = h_ref[...]
    lin = jnp.dot(h, wl_ref[...], preferred_element_type=F32)
    gate = jnp.dot(h, wg_ref[...], preferred_element_type=F32)
    o_ref[...] = lin * jax.nn.sigmoid(gate)


def _glu_call(h, w_in, layer):
    n, d = h.shape
    tm, tn = _token_tile(n, MM_TM), MM_TN
    goff = D_CONV // tn
    return pl.pallas_call(
        _glu_kernel,
        grid=(n // tm, D_CONV // tn),
        in_specs=[
            pl.BlockSpec((tm, d), lambda i, j: (i, 0)),
            pl.BlockSpec((None, d, tn), lambda i, j: (layer, 0, j)),
            pl.BlockSpec((None, d, tn), lambda i, j: (layer, 0, j + goff)),
        ],
        out_specs=pl.BlockSpec((tm, tn), lambda i, j: (i, j)),
        out_shape=jax.ShapeDtypeStruct((n, D_CONV), F32),
        compiler_params=_params("parallel", "arbitrary"),
        name="proj_glu",
    )(h, w_in, w_in)


def _head_proj_kernel(h_ref, w_ref, g_ref, *o_refs, norm, scale):
    z = jnp.dot(h_ref[...], w_ref[...], preferred_element_type=F32)
    for c in range(z.shape[1] // HEAD_DIM):
        cs = slice(c * HEAD_DIM, (c + 1) * HEAD_DIM)
        zc = z[:, cs]
        if norm:
            zc = zc * lax.rsqrt(jnp.mean(zc * zc, axis=-1, keepdims=True) + EPS) * g_ref[...]
        for o_ref in o_refs:
            o_ref[:, cs] = (zc * scale).astype(BF16) if o_ref.dtype == BF16 else zc


def _head_proj_call(h, w_in, layer, col0, g, *, norm, scale, out_dtypes):
    n, d = h.shape
    tm, tn = _token_tile(n, MM_TM), MM_TN
    j0 = col0 // tn
    tile = pl.BlockSpec((tm, tn), lambda i, j: (i, j))
    outs = pl.pallas_call(
        functools.partial(_head_proj_kernel, norm=norm, scale=scale),
        grid=(n // tm, D_ATT // tn),
        in_specs=[
            pl.BlockSpec((tm, d), lambda i, j: (i, 0)),
            pl.BlockSpec((None, d, tn), lambda i, j: (layer, 0, j + j0)),
            pl.BlockSpec((1, HEAD_DIM), lambda i, j: (0, 0)),
        ],
        out_specs=tuple(tile for _ in out_dtypes),
        out_shape=tuple(jax.ShapeDtypeStruct((n, D_ATT), dt) for dt in out_dtypes),
        compiler_params=_params("parallel", "arbitrary"),
        name="proj_heads",
    )(h, w_in, g)
    return outs


def _logf_kernel(h_ref, w_ref, b_ref, o_ref):
    z = jnp.dot(h_ref[...], w_ref[...], preferred_element_type=F32) + b_ref[...]
    o_ref[...] = jnp.minimum(z, 0.0) - jnp.log(1.0 + jnp.exp(-jnp.abs(z)))


def _logf_call(h, wf, bf, layer):
    n, d = h.shape
    tm = _token_tile(n, MM_TM)
    return pl.pallas_call(
        _logf_kernel,
        grid=(n // tm,),
        in_specs=[
            pl.BlockSpec((tm, d), lambda i: (i, 0)),
            pl.BlockSpec((None, d, LANES), lambda i: (layer, 0, 0)),
            pl.BlockSpec((None, 1, LANES), lambda i: (layer, 0, 0)),
        ],
        out_specs=pl.BlockSpec((tm, LANES), lambda i: (i, 0)),
        out_shape=jax.ShapeDtypeStruct((n, LANES), F32),
        compiler_params=_params("parallel"),
        name="proj_logf",
    )(h, wf, bf)


_HALO = 32


def _conv_kernel(a_ref, ap_ref, hist_ref, w_ref, g_ref, b_ref, o_ref, newc_ref, ext_ref, *, tt, n_tiles):
    i = pl.program_id(1)
    nh = CONV_WIDTH - 1
    ext_ref[0:_HALO - nh, :] = jnp.zeros((_HALO - nh, D_CONV), F32)
    ext_ref[_HALO:_HALO + tt, :] = a_ref[...]
    if n_tiles == 1:
        ext_ref[_HALO - nh:_HALO, :] = hist_ref[...]
    else:
        @pl.when(i == 0)
        def _():
            ext_ref[_HALO - nh:_HALO, :] = hist_ref[...]

        @pl.when(i > 0)
        def _():
            ext_ref[_HALO - nh:_HALO, :] = ap_ref[tt - nh:tt, :]

    cols = []
    for c in range(D_CONV // LANES):
        c0 = c * LANES
        win = ext_ref[:, c0:c0 + LANES]
        acc = jnp.zeros((tt, LANES), F32)
        for s in range(SUBLANES):
            sh = win if s == 0 else win[s:s + tt + _HALO - SUBLANES, :]
            for q in range(_HALO // SUBLANES + 1):
                j = SUBLANES * q + s - (_HALO - nh)
                if 0 <= j < CONV_WIDTH:
                    acc = acc + sh[SUBLANES * q:SUBLANES * q + tt, :] * w_ref[j:j + 1, c0:c0 + LANES]
        cols.append(acc)
    y = jnp.concatenate(cols, axis=1)
    mu = jnp.mean(y, axis=-1, keepdims=True)
    yc = y - mu
    z = yc * lax.rsqrt(jnp.mean(yc * yc, axis=-1, keepdims=True) + EPS) * g_ref[...] + b_ref[...]
    o_ref[...] = (z * jax.nn.sigmoid(z)).astype(o_ref.dtype)

    @pl.when(i == n_tiles - 1)
    def _():
        newc_ref[...] = ext_ref[tt + _HALO - nh:tt + _HALO, :]


def _conv_call(a, hist, conv_w, ln_g, ln_b):
    b, t, dc = a.shape
    tt = 256 if t % 256 == 0 else t
    n_tiles = t // tt
    nh = CONV_WIDTH - 1
    return pl.pallas_call(
        functools.partial(_conv_kernel, tt=tt, n_tiles=n_tiles),
        grid=(b, n_tiles),
        in_specs=[
            pl.BlockSpec((None, tt, dc), lambda bi, i: (bi, i, 0)),
            pl.BlockSpec((None, tt, dc), lambda bi, i: (bi, jnp.maximum(i - 1, 0), 0)),
            pl.BlockSpec((None, nh, dc), lambda bi, i: (bi, 0, 0)),
            pl.BlockSpec((CONV_WIDTH, dc), lambda bi, i: (0, 0)),
            pl.BlockSpec((1, dc), lambda bi, i: (0, 0)),
            pl.BlockSpec((1, dc), lambda bi, i: (0, 0)),
        ],
        out_specs=(
            pl.BlockSpec((None, tt, dc), lambda bi, i: (bi, i, 0)),
            pl.BlockSpec((None, nh, dc), lambda bi, i: (bi, 0, 0)),
        ),
        out_shape=(jax.ShapeDtypeStruct((b, t, dc), BF16), jax.ShapeDtypeStruct((b, nh, dc), F32)),
        scratch_shapes=[pltpu.VMEM((tt + _HALO, dc), F32)],
        compiler_params=_params("parallel", "arbitrary"),
        name="conv_branch",
    )(a, a, hist, conv_w, ln_g, ln_b)


_CS_BLK = 256


def _cumsum_kernel(x_ref, o_ref):
    t = x_ref.shape[0]
    r = lax.broadcasted_iota(jnp.int32, (_CS_BLK, _CS_BLK), 0)
    c = lax.broadcasted_iota(jnp.int32, (_CS_BLK, _CS_BLK), 1)
    tri = (c <= r).astype(F32)
    carry = jnp.zeros((1, x_ref.shape[1]), F32)
    for blk in range(t // _CS_BLK):
        rows = slice(blk * _CS_BLK, (blk + 1) * _CS_BLK)
        f = jnp.dot(tri, x_ref[rows, :], precision=lax.Precision.HIGHEST, preferred_element_type=F32) + carry
        o_ref[rows, :] = f
        carry = f[_CS_BLK - 1:_CS_BLK, :]


def _cumsum_call(x):
    b, t, w = x.shape
    return pl.pallas_call(
        _cumsum_kernel,
        grid=(b,),
        in_specs=[pl.BlockSpec((None, t, w), lambda bi: (bi, 0, 0))],
        out_specs=pl.BlockSpec((None, t, w), lambda bi: (bi, 0, 0)),
        out_shape=jax.ShapeDtypeStruct((b, t, w), F32),
        compiler_params=_params("parallel"),
        name="logf_cumsum",
    )(x)


ATT_TQ = 1024


def _attn_kernel(q_ref, k_ref, v_ref, fq_ref, fk_ref, o_ref, *, tq):
    i = pl.program_id(2)
    q = q_ref[...]
    fq = fq_ref[...]
    qpos = i * tq + lax.broadcasted_iota(jnp.int32, (tq, tq), 0)
    kidx = lax.broadcasted_iota(jnp.int32, (tq, tq), 1)

    def body(j, carry):
        m, l, acc = carry
        ks = pl.multiple_of(j * tq, tq)
        k = k_ref[pl.ds(ks, tq), :]
        v = v_ref[pl.ds(ks, tq), :]
        s = lax.dot_general(q, k, _NT, preferred_element_type=F32)
        s = s + fq - fk_ref[:, pl.ds(ks, tq)]
        s = jnp.where(ks + kidx <= qpos, s, -jnp.inf)
        m_new = jnp.maximum(m, jnp.max(s, axis=1, keepdims=True))
        p = jnp.exp(s - m_new)
        alpha = jnp.exp(m - m_new)
        l = alpha * l + jnp.sum(p, axis=1, keepdims=True)
        acc = alpha * acc + jnp.dot(p.astype(BF16), v, preferred_element_type=F32)
        return m_new, l, acc

    init = (jnp.full((tq, 1), -jnp.inf, F32), jnp.zeros((tq, 1), F32), jnp.zeros((tq, HEAD_DIM), F32))
    _, l, acc = lax.fori_loop(0, i + 1, body, init)
    o_ref[...] = (acc / l).astype(o_ref.dtype)


def _attn_prompt_call(q, k, v, f_col, f_row, batch, t):
    tq = min(ATT_TQ, t)
    nq = t // tq
    return pl.pallas_call(
        functools.partial(_attn_kernel, tq=tq),
        grid=(batch, N_HEADS, nq),
        in_specs=[
            pl.BlockSpec((tq, HEAD_DIM), lambda b, h, i: (b * nq + i, h)),
            pl.BlockSpec((t, HEAD_DIM), lambda b, h, i: (b, h)),
            pl.BlockSpec((t, HEAD_DIM), lambda b, h, i: (b, h)),
            pl.BlockSpec((None, None, tq, 1), lambda b, h, i: (b, h, i, 0)),
            pl.BlockSpec((None, None, 1, t), lambda b, h, i: (b, h, 0, 0)),
        ],
        out_specs=pl.BlockSpec((tq, HEAD_DIM), lambda b, h, i: (b * nq + i, h)),
        out_shape=jax.ShapeDtypeStruct((batch * t, D_ATT), BF16),
        compiler_params=_params("parallel", "parallel", "arbitrary"),
        name="fox_prompt",
    )(q, k, v, f_col, f_row)


ATT_PAGES = 8


def _sample_attn_kernel(pt_ref, qt_ref, kn_ref, vn_ref, lfn_ref, *rest, n_new, n_steps):
    kp = rest[0:ATT_PAGES]
    vp = rest[ATT_PAGES:2 * ATT_PAGES]
    lp = rest[2 * ATT_PAGES:3 * ATT_PAGES]
    o_ref, m_ref, l_ref, acc_ref, carry_ref, fq_ref, mask_ref, tri_ref = rest[3 * ATT_PAGES:]
    g = pl.program_id(1)
    cols = qt_ref.shape[1]
    page = kp[0].shape[0]
    rows = page * N_HEADS
    new_rows = n_new * N_HEADS

    nr = lax.broadcasted_iota(jnp.int32, (new_rows, cols), 0)
    nc = lax.broadcasted_iota(jnp.int32, (new_rows, cols), 1)
    new_s, new_h = nr // N_HEADS, nr % N_HEADS
    col_t, col_h = nc // N_HEADS, nc % N_HEADS

    @pl.when(g == 0)
    def _():
        fs = jnp.zeros((new_rows, cols), F32)
        for t in range(n_new):
            fs = fs + jnp.where(new_s >= t, lfn_ref[t:t + 1, :], 0.0)
        fq_ref[0:new_rows, :] = fs
        pick = jnp.where(new_s == col_t, jnp.where(new_h == 0, fs, 0.0), 0.0)
        fq_ref[new_rows:new_rows + 1, :] = jnp.sum(pick, axis=0, keepdims=True)
        m_ref[...] = jnp.full(m_ref.shape, -jnp.inf, F32)
        l_ref[...] = jnp.zeros(l_ref.shape, F32)
        acc_ref[...] = jnp.zeros(acc_ref.shape, F32)
        carry_ref[...] = jnp.zeros(carry_ref.shape, F32)
        r = lax.broadcasted_iota(jnp.int32, (rows, cols), 0)
        c = lax.broadcasted_iota(jnp.int32, (rows, cols), 1)
        mask_ref[...] = jnp.where(r % N_HEADS == c % N_HEADS, 0.0, -jnp.inf)
        tr = lax.broadcasted_iota(jnp.int32, (page, page), 0)
        tc = lax.broadcasted_iota(jnp.int32, (page, page), 1)
        tri_ref[...] = jnp.where(tc > tr, 1.0, 0.0).astype(BF16)

    qt = qt_ref[...]
    fq = fq_ref[new_rows:new_rows + 1, :]

    def update(scores, values):
        m_old = m_ref[...]
        m_new = m_old
        for s in scores:
            m_new = jnp.maximum(m_new, jnp.max(s, axis=0, keepdims=True))
        alpha = jnp.exp(m_old - m_new)
        l_new = alpha * l_ref[...]
        acc = alpha * acc_ref[...]
        for s, v_bf in zip(scores, values):
            p = jnp.exp(s - m_new)
            l_new = l_new + jnp.sum(p, axis=0, keepdims=True)
            acc = acc + lax.dot_general(v_bf, p.astype(BF16), _TN, preferred_element_type=F32)
        l_ref[...] = l_new
        acc_ref[...] = acc
        m_ref[...] = m_new

    tri = tri_ref[...]
    carry = carry_ref[...]
    scores, values = [], []
    for r in range(ATT_PAGES):
        lpv = lp[r][...]
        hi = lpv.astype(BF16)
        rem = lpv - hi.astype(F32)
        mid = rem.astype(BF16)
        lo = (rem - mid.astype(F32)).astype(BF16)
        after = (jnp.dot(tri, hi, preferred_element_type=F32) + jnp.dot(tri, mid, preferred_element_type=F32)
                 + jnp.dot(tri, lo, preferred_element_type=F32))
        bias = after + (carry + fq)
        bias = jnp.broadcast_to(bias[:, None, :], (page, N_HEADS, cols)).reshape(rows, cols)
        k2 = kp[r][...].reshape(rows, HEAD_DIM).astype(BF16)
        s = jnp.dot(k2, qt, preferred_element_type=F32)
        scores.append(s + (bias + mask_ref[...]))
        values.append(vp[r][...].reshape(rows, HEAD_DIM).astype(BF16))
        carry = carry + after[0:1, :] + lpv[0:1, :]
    update(scores, values)
    carry_ref[...] = carry

    @pl.when(g == n_steps - 1)
    def _():
        s = jnp.dot(kn_ref[...], qt, preferred_element_type=F32)
        s = s + fq - fq_ref[0:new_rows, :]
        s = jnp.where(new_h == col_h, jnp.where(new_s <= col_t, s, -jnp.inf), -jnp.inf)
        update([s], [vn_ref[...]])
        o_ref[...] = acc_ref[...] / l_ref[...]


def _sample_attn_call(page_table, qt, k_new, v_new, lf_new, cache_k, cache_v, cache_lf, layer, n_new):
    bd, n_pages = page_table.shape
    page = cache_k.shape[2]
    cols = qt.shape[2]
    new_rows = n_new * N_HEADS
    n_steps = n_pages // ATT_PAGES
    lf_rows = lf_new.shape[1]

    def kv_map(r):
        return lambda b, g, pt: (layer, pt[b, n_pages - 1 - (g * ATT_PAGES + r)], 0, 0, 0)

    def lf_map(r):
        return lambda b, g, pt: (layer, pt[b, n_pages - 1 - (g * ATT_PAGES + r)], 0, 0)

    per_b = lambda b, g, pt: (b, 0, 0)
    in_specs = [
        pl.BlockSpec((None, HEAD_DIM, cols), per_b),
        pl.BlockSpec((None, new_rows, HEAD_DIM), per_b),
        pl.BlockSpec((None, new_rows, HEAD_DIM), per_b),
        pl.BlockSpec((None, lf_rows, cols), per_b),
    ]
    in_specs += [pl.BlockSpec((None, None, page, N_HEADS, HEAD_DIM), kv_map(r)) for r in range(ATT_PAGES)]
    in_specs += [pl.BlockSpec((None, None, page, N_HEADS, HEAD_DIM), kv_map(r)) for r in range(ATT_PAGES)]
    in_specs += [pl.BlockSpec((None, None, page, cols), lf_map(r)) for r in range(ATT_PAGES)]
    grid_spec = pltpu.PrefetchScalarGridSpec(
        num_scalar_prefetch=1,
        grid=(bd, n_steps),
        in_specs=in_specs,
        out_specs=pl.BlockSpec((None, HEAD_DIM, cols), per_b),
        scratch_shapes=[
            pltpu.VMEM((1, cols), F32),
            pltpu.VMEM((1, cols), F32),
            pltpu.VMEM((HEAD_DIM, cols), F32),
            pltpu.VMEM((1, cols), F32),
            pltpu.VMEM((new_rows + SUBLANES, cols), F32),
            pltpu.VMEM((page * N_HEADS, cols), F32),
            pltpu.VMEM((page, page), BF16),
        ],
    )
    return pl.pallas_call(
        functools.partial(_sample_attn_kernel, n_new=n_new, n_steps=n_steps),
        grid_spec=grid_spec,
        out_shape=jax.ShapeDtypeStruct((bd, HEAD_DIM, cols), F32),
        compiler_params=_params("parallel", "arbitrary"),
        name="fox_sample",
    )(page_table, qt, k_new, v_new, lf_new,
      *([cache_k] * ATT_PAGES), *([cache_v] * ATT_PAGES), *([cache_lf] * ATT_PAGES))


def _gated_kernel(h_ref, wg_ref, a_ref, w_ref, *rest):
    gate = jax.nn.sigmoid(jnp.dot(h_ref[...], wg_ref[...], preferred_element_type=F32))
    y = gate * jnp.dot(a_ref[...], w_ref[...], preferred_element_type=F32)
    if len(rest) == 2:
        y = y + rest[0][...]
    o_ref = rest[-1]
    o_ref[...] = y.astype(o_ref.dtype)


def _gated_call(h, w_gates, layer, gate_col0, a, w, prev, out_dtype):
    n, d = h.shape
    ka = a.shape[1]
    tm, tn = _token_tile(n, MM_TM), MM_TN
    j0 = gate_col0 // tn
    tile = pl.BlockSpec((tm, tn), lambda i, j: (i, j))
    in_specs = [
        pl.BlockSpec((tm, d), lambda i, j: (i, 0)),
        pl.BlockSpec((None, d, tn), lambda i, j: (layer, 0, j + j0)),
        pl.BlockSpec((tm, ka), lambda i, j: (i, 0)),
        pl.BlockSpec((None, ka, tn), lambda i, j: (layer, 0, j)),
    ]
    args = [h, w_gates, a, w]
    if prev is not None:
        in_specs.append(tile)
        args.append(prev)
    return pl.pallas_call(
        _gated_kernel,
        grid=(n // tm, D_MODEL // tn),
        in_specs=in_specs,
        out_specs=tile,
        out_shape=jax.ShapeDtypeStruct((n, D_MODEL), out_dtype),
        compiler_params=_params("parallel", "arbitrary"),
        name="gated_branch",
    )(*args)


def _out_proj_kernel(m_ref, w_ref, x_ref, gt_ref, o_ref):
    o_ref[...] = x_ref[...] + gt_ref[...] * jnp.dot(m_ref[...], w_ref[...], preferred_element_type=F32)


def _out_proj_call(m, w_out, layer, x, gt, tokens_per_group):
    n, d = m.shape
    tm, tn = _token_tile(n, MM_TM), MM_TN
    tpg = max(tokens_per_group // tm, 1)
    tile = pl.BlockSpec((tm, tn), lambda i, j: (i, j))
    return pl.pallas_call(
        _out_proj_kernel,
        grid=(n // tm, D_MODEL // tn),
        in_specs=[
            pl.BlockSpec((tm, d), lambda i, j: (i, 0)),
            pl.BlockSpec((None, d, tn), lambda i, j: (layer, 0, j)),
            tile,
            _row_spec(gt, tm, tpg, tn, col=True),
        ],
        out_specs=tile,
        out_shape=jax.ShapeDtypeStruct((n, D_MODEL), F32),
        compiler_params=_params("parallel", "arbitrary"),
        name="out_proj",
    )(m, w_out, x, gt)


def _peer_score_kernel(h_ref, w_ref, sk_ref, o_ref):
    z = jnp.dot(h_ref[...], w_ref[...], preferred_element_type=F32).astype(BF16)
    for p in range(2):
        zp = z[:, p * N_KEYS:(p + 1) * N_KEYS]
        o_ref[p] = lax.dot_general(sk_ref[p], zp, _NT, preferred_element_type=F32)


def _peer_score_call(h, w_pq, sub_keys, layer):
    n, d = h.shape
    tm = _token_tile(n, MM_TM)
    dk = 2 * N_KEYS
    return pl.pallas_call(
        _peer_score_kernel,
        grid=(n // tm, PEER_HEADS),
        in_specs=[
            pl.BlockSpec((tm, d), lambda i, j: (i, 0)),
            pl.BlockSpec((None, d, dk), lambda i, j: (layer, 0, j)),
            pl.BlockSpec((None, 2, N_KEYS, N_KEYS), lambda i, j: (layer, j, 0, 0)),
        ],
        out_specs=pl.BlockSpec((2, N_KEYS, tm), lambda i, j: (j, 0, i)),
        out_shape=jax.ShapeDtypeStruct((2 * PEER_HEADS, N_KEYS, n), F32),
        compiler_params=_params("parallel", "arbitrary"),
        name="peer_scores",
    )(h, w_pq, sub_keys)


def _knock_out_max(s, rid):
    mx = jnp.max(s, axis=0, keepdims=True)
    pos = jnp.min(jnp.where(s == mx, rid, s.shape[0]), axis=0, keepdims=True)
    hit = rid == pos
    return mx, jnp.where(hit, -jnp.inf, s), hit


def _top_desc(s, k, want_rank):
    rid = lax.broadcasted_iota(jnp.int32, s.shape, 0)
    rk = lax.broadcasted_iota(jnp.int32, (k, s.shape[1]), 0)
    tops = jnp.full((k, s.shape[1]), -jnp.inf, F32)
    rank = jnp.full(s.shape, float(k), F32)
    for r in range(k):
        mx, s, hit = _knock_out_max(s, rid)
        tops = jnp.where(rk == r, mx, tops)
        if want_rank:
            rank = jnp.where(hit, float(r), rank)
    return tops, rank


TOPK_HEADS_PER_ITER = 4


def _peer_topk_kernel(st_ref, rank_ref, p2_ref, cnt_ref, c1_ref):
    cw = st_ref.shape[2]
    k = PEER_TOPK
    row8 = lax.broadcasted_iota(jnp.int32, (SUBLANES, cw), 0)

    def one_head(h):
        s1 = st_ref[2 * h]
        s2 = st_ref[2 * h + 1]
        t1, _ = _top_desc(s1, k, False)
        t2, rank2 = _top_desc(s2, k, True)
        blocks = [t2 + t1[0:1, :], t2[0:SUBLANES, :] + t1[1:2, :]]
        for a in range(2, SUBLANES):
            blocks.append(jnp.where(row8 < k // (a + 1), t2[0:SUBLANES, :] + t1[a:a + 1, :], -jnp.inf))
        blocks.append(t1[SUBLANES:k, :] + t2[0:1, :])
        cand = jnp.concatenate(blocks, axis=0)
        rid = lax.broadcasted_iota(jnp.int32, cand.shape, 0)
        work = cand
        for _ in range(k):
            tau, work, _ = _knock_out_max(work, rid)
        m1 = t1[0:1, :]
        m2 = t2[0:1, :]
        z = jnp.sum(jnp.where(cand >= tau, jnp.exp(cand - (m1 + m2)), 0.0), axis=0, keepdims=True)
        cnt = jnp.zeros(s1.shape, F32)
        for b in range(k):
            cnt = cnt + jnp.where(s1 + t2[b:b + 1, :] >= tau, 1.0, 0.0)
        rank_ref[h] = pltpu.bitcast(rank2.astype(BF16), jnp.uint32)
        p2_ref[h] = pltpu.bitcast(jnp.exp(s2 - m2).astype(BF16), jnp.uint32)
        cnt_ref[h] = cnt
        c1_ref[h] = jnp.exp(s1 - m1) / z

    def head_group(hg, carry):
        for j in range(TOPK_HEADS_PER_ITER):
            one_head(TOPK_HEADS_PER_ITER * hg + j)
        return carry

    lax.fori_loop(0, PEER_HEADS // TOPK_HEADS_PER_ITER, head_group, 0)


def _peer_topk_call(st):
    n = st.shape[2]
    cw = LANES if n % LANES == 0 else n
    spec = pl.BlockSpec((PEER_HEADS, N_KEYS, cw), lambda i: (0, 0, i))
    spec_pk = pl.BlockSpec((PEER_HEADS, N_KEYS // 2, cw), lambda i: (0, 0, i))
    shape = lambda dt: jax.ShapeDtypeStruct((PEER_HEADS, N_KEYS, n), dt)
    shape_pk = jax.ShapeDtypeStruct((PEER_HEADS, N_KEYS // 2, n), jnp.uint32)
    return pl.pallas_call(
        _peer_topk_kernel,
        grid=(n // cw,),
        in_specs=[pl.BlockSpec((2 * PEER_HEADS, N_KEYS, cw), lambda i: (0, 0, i))],
        out_specs=(spec_pk, spec_pk, spec, spec),
        out_shape=(shape_pk, shape_pk, shape(F32), shape(F32)),
        compiler_params=_params("parallel"),
        name="peer_topk",
    )(st)


PEER_TM = 512
PEER_E1 = 8
PACK = 16

_INV_SQRT2 = 0.7071067811865476


def _peer_mix_kernel(h_ref, u_ref, v_ref, rank_ref, p2_ref, cnt_ref, c1_ref, o_ref, act_ref, wact_ref, *, cw):
    k = pl.program_id(1)
    tm = h_ref.shape[0]

    @pl.when(k == 0)
    def _():
        o_ref[...] = jnp.zeros(o_ref.shape, F32)

    act_ref[...] = lax.dot_general(u_ref[...], h_ref[...], _NT, preferred_element_type=F32)
    zero = jnp.zeros((), BF16)
    for g in range(PEER_E1):
        e1 = k * PEER_E1 + g
        for ch in range(tm // cw):
            cs = slice(ch * cw, (ch + 1) * cw)
            cnt8 = cnt_ref[e1, :, cs]
            c18 = c1_ref[e1, :, cs]
            cnt_b = [jnp.broadcast_to(cnt8[h:h + 1, :], (PACK, cw)).astype(BF16) for h in range(PEER_HEADS)]
            c1_b = [jnp.broadcast_to(c18[h:h + 1, :], (PACK, cw)).astype(BF16) for h in range(PEER_HEADS)]
            for rg in range(N_KEYS // PACK):
                rs = slice(rg * PACK // 2, (rg + 1) * PACK // 2)
                w = jnp.zeros((PACK, cw), BF16)
                for h in range(PEER_HEADS):
                    rank = pltpu.bitcast(rank_ref[h, rs, cs], BF16)
                    p2 = pltpu.bitcast(p2_ref[h, rs, cs], BF16)
                    w = w + jnp.where(rank < cnt_b[h], p2 * c1_b[h], zero)
                ers = slice(g * N_KEYS + rg * PACK, g * N_KEYS + (rg + 1) * PACK)
                a = act_ref[ers, cs]
                gelu = 0.5 * a * (1.0 + lax.erf(a * _INV_SQRT2))
                wact_ref[ers, cs] = w * gelu.astype(BF16)
    o_ref[...] += lax.dot_general(wact_ref[...], v_ref[...], _TN, preferred_element_type=F32)


def _peer_mix_call(h, u, v, layer, rank2, p2, cnt, c1):
    n, d = h.shape
    n_exp = u.shape[1]
    tm = _token_tile(n, PEER_TM)
    cw = LANES if tm % LANES == 0 else tm
    eb = PEER_E1 * N_KEYS
    n_blk = n_exp // eb
    table = pl.BlockSpec((PEER_HEADS, N_KEYS // 2, tm), lambda i, k: (0, 0, i))
    table_e1 = pl.BlockSpec((N_KEYS, PEER_HEADS, tm), lambda i, k: (0, 0, i))
    return pl.pallas_call(
        functools.partial(_peer_mix_kernel, cw=cw),
        grid=(n // tm, n_blk),
        in_specs=[
            pl.BlockSpec((tm, d), lambda i, k: (i, 0)),
            pl.BlockSpec((None, eb, d), lambda i, k: (layer, k, 0)),
            pl.BlockSpec((None, eb, d), lambda i, k: (layer, k, 0)),
            table, table, table_e1, table_e1,
        ],
        out_specs=pl.BlockSpec((tm, d), lambda i, k: (i, 0)),
        out_shape=jax.ShapeDtypeStruct((n, d), F32),
        scratch_shapes=[pltpu.VMEM((eb, tm), F32), pltpu.VMEM((eb, tm), BF16)],
        compiler_params=_params("parallel", "arbitrary"),
        name="peer_mix",
    )(h, u, v, rank2, p2, cnt, c1)


def _layer(x, pending, mods, hist, lw, sw, layer, batch, t, attend):
    sh1, sc1, gt1, sh2, sc2, gt2 = mods
    if pending is None:
        _, h1 = _norm_call(x, lw["g_norm1"], sc1, sh1, t)
    else:
        x, h1 = _norm_call(x, lw["g_norm1"], sc1, sh1, t, pending)
    a = _glu_call(h1, sw["w_in"], layer)
    scale = HEAD_DIM ** -0.5
    (q_bf,) = _head_proj_call(h1, sw["w_in"], layer, 2 * D_CONV, lw["g_q"], norm=True, scale=scale,
                              out_dtypes=(BF16,))
    k_f32, k_bf = _head_proj_call(h1, sw["w_in"], layer, 2 * D_CONV + D_ATT, lw["g_k"], norm=True, scale=1.0,
                                  out_dtypes=(F32, BF16))
    v_f32, v_bf = _head_proj_call(h1, sw["w_in"], layer, 2 * D_CONV + 2 * D_ATT, lw["g_k"], norm=False, scale=1.0,
                                  out_dtypes=(F32, BF16))
    logf = _logf_call(h1, sw["w_f"], sw["b_f"], layer)
    act, new_conv = _conv_call(a.reshape(batch, t, D_CONV), hist, lw["conv_w"], lw["conv_ln_g"], lw["conv_ln_b"])
    o = attend(q_bf, k_bf, v_bf, logf)
    part = _gated_call(h1, sw["w_gates"], layer, 0, act.reshape(batch * t, D_CONV), sw["w_a_proj"], None, F32)
    mix = _gated_call(h1, sw["w_gates"], layer, D_MODEL, o, sw["w_b_proj"], part, BF16)
    x = _out_proj_call(mix, sw["w_out"], layer, x, gt1, t)
    _, h2 = _norm_call(x, lw["g_norm2"], sc2, sh2, t)
    st = _peer_score_call(h2, sw["w_pq"], sw["sub_keys"], layer)
    rank2, p2, cnt, c1 = _peer_topk_call(st)
    y = _peer_mix_call(h2, sw["peer_u"], sw["peer_v"], layer, rank2, p2,
                       cnt.transpose(1, 0, 2), c1.transpose(1, 0, 2))
    return x, (y, gt2), k_f32, v_f32, logf[:, :N_HEADS], new_conv


def kernel(x_prompt, x_sample, c_prompt, c_sample, cache_k, cache_v, cache_logf, state_conv, page_table, w_ada, b_ada, g_norm1, g_norm2, w_in, b_f, g_q, g_k, conv_w, conv_ln_g, conv_ln_b, w_a_proj, w_b_proj, w_out, w_pq, sub_keys, peer_u, peer_v):
    bp, seq, d = x_prompt.shape
    bd, tdec, _ = x_sample.shape
    depth = w_ada.shape[0]
    cols = tdec * N_HEADS

    rows = bp + bd
    rows_pad = -(-rows // SUBLANES) * SUBLANES
    c_all = jnp.concatenate([c_prompt, c_sample, jnp.zeros((rows_pad - rows, d), F32)], axis=0)
    mod = _ada_call(c_all, w_ada, b_ada)

    c_f = 2 * D_CONV + 3 * D_ATT
    w_in_bf = w_in.astype(BF16)
    w_gates = w_in[:, :, c_f + N_HEADS:].astype(BF16)
    w_f = jnp.pad(w_in[:, :, c_f:c_f + N_HEADS], ((0, 0), (0, 0), (0, LANES - N_HEADS))).astype(BF16)
    b_f_pad = jnp.pad(b_f, ((0, 0), (0, LANES - N_HEADS))).reshape(depth, 1, LANES)
    w_a_bf, w_b_bf, w_out_bf, w_pq_bf = (w.astype(BF16) for w in (w_a_proj, w_b_proj, w_out, w_pq))
    sk_bf = sub_keys.reshape(depth, 2 * PEER_HEADS, N_KEYS, N_KEYS).astype(BF16)
    sw = dict(w_in=w_in_bf, w_gates=w_gates, w_f=w_f, b_f=b_f_pad, w_a_proj=w_a_bf, w_b_proj=w_b_bf,
              w_out=w_out_bf, w_pq=w_pq_bf, sub_keys=sk_bf, peer_u=peer_u.astype(BF16), peer_v=peer_v.astype(BF16))

    clf = jnp.tile(cache_logf, (1, 1, 1, tdec))

    def attend_prompt(q_bf, k_bf, v_bf, logf):
        f = _cumsum_call(logf.reshape(bp, seq, LANES))[:, :, :N_HEADS]
        f_t = f.transpose(0, 2, 1)
        return _attn_prompt_call(q_bf, k_bf, v_bf, f_t[..., None], f_t[:, :, None, :], bp, seq)

    def make_attend_sample(l):
        def attend_sample(q_bf, k_bf, v_bf, logf):
            qt = q_bf.reshape(bd, cols, HEAD_DIM).transpose(0, 2, 1)
            kn = k_bf.reshape(bd, cols, HEAD_DIM)
            vn = v_bf.reshape(bd, cols, HEAD_DIM)
            lfn = jnp.tile(logf[:, :N_HEADS].reshape(bd, tdec, N_HEADS), (1, 1, tdec))
            lfn = jnp.pad(lfn, ((0, 0), (0, SUBLANES - tdec), (0, 0)))
            o_t = _sample_attn_call(page_table, qt, kn, vn, lfn, cache_k, cache_v, clf, l, tdec)
            return o_t.transpose(0, 2, 1).reshape(bd * tdec, D_ATT).astype(BF16)
        return attend_sample

    xp = x_prompt.reshape(bp * seq, d)
    xs = x_sample.reshape(bd * tdec, d)
    pend_p = pend_s = None
    zero_hist = jnp.zeros((bp, CONV_WIDTH - 1, D_CONV), F32)
    outs_p, outs_s = [], []
    for l in range(depth):
        lw = dict(g_norm1=g_norm1[l][None], g_norm2=g_norm2[l][None], g_q=g_q[l][None], g_k=g_k[l][None],
                  conv_w=conv_w[l], conv_ln_g=conv_ln_g[l][None], conv_ln_b=conv_ln_b[l][None])
        mods_p = tuple(m[:, None, :] for m in jnp.split(mod[l, :bp], 6, axis=-1))
        mods_s = tuple(jnp.repeat(m, tdec, axis=0)[None] for m in jnp.split(mod[l, bp:rows], 6, axis=-1))
        xp, pend_p, kp, vp, fp, cp = _layer(xp, pend_p, mods_p, zero_hist, lw, sw, l, bp, seq, attend_prompt)
        xs, pend_s, ks, vs, fs, cs = _layer(xs, pend_s, mods_s, state_conv[l], lw, sw, l, bd, tdec, make_attend_sample(l))
        outs_p.append((kp.reshape(bp, seq, N_HEADS, HEAD_DIM), vp.reshape(bp, seq, N_HEADS, HEAD_DIM),
                       fp.reshape(bp, seq, N_HEADS), cp))
        outs_s.append((ks.reshape(bd, tdec, N_HEADS, HEAD_DIM), vs.reshape(bd, tdec, N_HEADS, HEAD_DIM),
                       fs.reshape(bd, tdec, N_HEADS), cs))
    xp = _resid_call(xp, pend_p[0], pend_p[1], seq).reshape(bp, seq, d)
    xs = _resid_call(xs, pend_s[0], pend_s[1], tdec).reshape(bd, tdec, d)
    stack = lambda outs, i: jnp.stack([o[i] for o in outs])
    return (xp, xs, stack(outs_p, 0), stack(outs_p, 1), stack(outs_p, 2), stack(outs_p, 3),
            stack(outs_s, 0), stack(outs_s, 1), stack(outs_s, 2), stack(outs_s, 3))
```
